```python
import jax, jax.numpy as jnp
from jax import lax
import numpy as np

D_MODEL = 2048
BATCH = 2
SEQ = 8192
DEPTH = 4
DEC_BATCH = 32
DEC_SEQ = 16
PAST_LEN = 4096

CHUNK = 64
D_CONV = 512
D_POOL = 512
D_SSM = 1024
CONV_WIDTH = 31
POOL_GROUPS = 4
POOL_WINDOWS = (2, 4, 8, 16)
POOL_GROUP_DIM = D_POOL // POOL_GROUPS
POOL_HIST = max(POOL_WINDOWS) - 1
SSM_HEAD_DIM = 64
SSM_HEADS = D_SSM // SSM_HEAD_DIM
SSM_GROUPS = 2
SSM_HEADS_PER_GROUP = SSM_HEADS // SSM_GROUPS
SSM_STATE = 128
SSM_CONV_WIDTH = 4
SSM_BC = SSM_GROUPS * SSM_STATE
SSM_CONV_DIM = D_SSM + 2 * SSM_BC
SSD_CHUNK = CHUNK
N_BRANCH = 3
IN_SIZES = (D_CONV, D_CONV, D_POOL, D_SSM, SSM_CONV_DIM, SSM_HEADS, N_BRANCH * D_MODEL)
IN_COLS = sum(IN_SIZES)
PEER_HEADS = 8
PEER_KEYS = 128
PEER_EXPERTS = PEER_KEYS * PEER_KEYS
PEER_DK = 256
PEER_TOPK = 16
PEER_BLOCK = 128
ALPHA = (2 * DEPTH) ** 0.25
BETA = (8 * DEPTH) ** -0.25
LN_EPS = 1e-5

kernel_name = 'streaming_conv_pool_ssd_peer_trunk_step'

F32 = jnp.float32


def layer_norm(x, g=None, b=None):
    x32 = x.astype(F32)
    mu = jnp.mean(x32, axis=-1, keepdims=True)
    var = jnp.mean(jnp.square(x32 - mu), axis=-1, keepdims=True)
    y = (x32 - mu) * lax.rsqrt(var + LN_EPS)
    if g is not None:
        y = y * g.astype(F32) + b.astype(F32)
    return y.astype(x.dtype)


def rms_norm(x32, g):
    return x32 * lax.rsqrt(jnp.mean(jnp.square(x32), axis=-1, keepdims=True) + LN_EPS) * g.astype(F32)


def causal_dwconv(u, hist, w, b):
    ext = jnp.concatenate([hist.astype(u.dtype), u], axis=1)
    y = lax.conv_general_dilated(ext, w[:, None, :].astype(u.dtype), window_strides=(1,), padding='VALID',
                                 dimension_numbers=('NWC', 'WIO', 'NWC'), feature_group_count=u.shape[-1])
    return y + b, ext[:, -(w.shape[0] - 1):]


def pool_mixer(p, hist, start, pool_w, pool_scale):
    b, L, _ = p.shape
    ext = jnp.concatenate([hist.astype(p.dtype), p], axis=1)
    ext32 = ext.astype(F32)
    cs = jnp.cumsum(jnp.pad(ext32, ((0, 0), (1, 0), (0, 0))), axis=1)
    cur = cs[:, POOL_HIST + 1:]
    pos = start + jnp.arange(L) + 1
    outs = []
    for gi, w in enumerate(POOL_WINDOWS):
        sl = slice(gi * POOL_GROUP_DIM, (gi + 1) * POOL_GROUP_DIM)
        win = cur[:, :, sl] - cs[:, POOL_HIST + 1 - w:POOL_HIST + 1 - w + L, sl]
        cnt = jnp.minimum(pos, w).astype(F32)[None, :, None]
        outs.append(win / cnt - ext32[:, POOL_HIST:, sl])
    mix = jnp.stack(outs, axis=2)
    y = jnp.einsum('blgc,gcd->blgd', mix, pool_w.astype(F32)).reshape(b, L, D_POOL) * pool_scale.astype(F32)
    return y.astype(p.dtype), ext[:, -POOL_HIST:]


def ssd_scan(x, dt, a, bm, cm, h0):
    b, L = x.shape[0], x.shape[1]
    Q = min(SSD_CHUNK, L)
    nc = L // Q
    G, HG, P, N = SSM_GROUPS, SSM_HEADS_PER_GROUP, SSM_HEAD_DIM, SSM_STATE
    x = x.astype(F32).reshape(b, nc, Q, G, HG, P)
    dt = dt.reshape(b, nc, Q, G, HG)
    bm = bm.astype(F32).reshape(b, nc, Q, G, N)
    cm = cm.astype(F32).reshape(b, nc, Q, G, N)
    a_cs = jnp.cumsum(dt * a.reshape(G, HG), axis=2)
    seg = a_cs[:, :, :, None] - a_cs[:, :, None, :]
    causal = jnp.tril(jnp.ones((Q, Q), bool))[:, :, None, None]
    decay = jnp.exp(jnp.where(causal, seg, -jnp.inf))
    dtx = dt[..., None] * x
    cb = jnp.einsum('bcign,bcjgn->bcijg', cm, bm)
    y_diag = jnp.einsum('bcijg,bcijgh,bcjghp->bcighp', cb, decay, dtx)
    to_end = jnp.exp(a_cs[:, :, -1:] - a_cs)
    states = jnp.einsum('bcjgn,bcjghp->bcghpn', bm, to_end[..., None] * dtx)
    chunk_decay = jnp.exp(a_cs[:, :, -1])

    def step(h, inp):
        dec, st = inp
        return h * dec[..., None, None] + st, h

    h_last, h_in = lax.scan(step, h0.astype(F32).reshape(b, G, HG, P, N),
                            (jnp.moveaxis(chunk_decay, 1, 0), jnp.moveaxis(states, 1, 0)))
    h_in = jnp.moveaxis(h_in, 0, 1)
    y_off = jnp.einsum('bcign,bcghpn->bcighp', cm, h_in) * jnp.exp(a_cs)[..., None]
    y = (y_diag + y_off).reshape(b, L, SSM_HEADS, P)
    return y, h_last.reshape(b, SSM_HEADS, P, N)


def ssd_mixer(z, xbc, dt_raw, conv_hist, h0, conv_w, conv_b, dt_bias, a_log, d_skip, norm_g):
    b, L, _ = z.shape
    xbc, new_conv = causal_dwconv(xbc, conv_hist, conv_w, conv_b)
    xbc = jax.nn.silu(xbc)
    xs = xbc[..., :D_SSM].reshape(b, L, SSM_HEADS, SSM_HEAD_DIM)
    bm = xbc[..., D_SSM:D_SSM + SSM_BC].reshape(b, L, SSM_GROUPS, SSM_STATE)
    cm = xbc[..., D_SSM + SSM_BC:].reshape(b, L, SSM_GROUPS, SSM_STATE)
    dt = jax.nn.softplus(dt_raw.astype(F32) + dt_bias.astype(F32))
    a = -jnp.exp(a_log.astype(F32))
    y, h_last = ssd_scan(xs, dt, a, bm, cm, h0)
    y = y + d_skip.astype(F32)[:, None] * xs.astype(F32)
    y = y.reshape(b, L, D_SSM) * jax.nn.silu(z.astype(F32))
    return rms_norm(y, norm_g).astype(z.dtype), new_conv, h_last.astype(h0.dtype)


def token_mixer(h, start, hist_conv, hist_pool, hist_sconv, h0, pl):
    idx, acc = [], 0
    for s in IN_SIZES[:-1]:
        acc += s
        idx.append(acc)
    a_val, a_gate, pb, z, xbc, dt_raw, gates = jnp.split(h @ pl['w_in'], idx, axis=-1)
    glu = a_val * jax.nn.sigmoid(a_gate)
    ya, new_conv = causal_dwconv(glu, hist_conv, pl['conv_a_w'], pl['conv_a_b'])
    ya = jax.nn.silu(layer_norm(ya, pl['ln_a_g'], pl['ln_a_b'])) @ pl['w_a_out']
    yb, new_pool = pool_mixer(pb, hist_pool, start, pl['pool_w'], pl['pool_scale'])
    yb = yb @ pl['w_b_out']
    yc, new_sconv, new_ssm = ssd_mixer(z, xbc, dt_raw, hist_sconv, h0, pl['ssm_conv_w'], pl['ssm_conv_b'],
                                       pl['dt_bias'], pl['a_log'], pl['d_skip'], pl['ssm_norm_g'])
    yc = yc @ pl['w_c_out']
    g_a, g_b, g_c = jnp.split(jax.nn.sigmoid(gates), N_BRANCH, axis=-1)
    out = (g_a * ya + g_b * yb + g_c * yc) @ pl['w_o']
    return out, new_conv, new_pool, new_sconv, new_ssm


def peer(h, wq, keys, u_tab, v_tab):
    b, L, D = h.shape
    T = b * L
    hf = h.reshape(T, D)
    q = (hf @ wq).reshape(T, PEER_HEADS, 2, PEER_DK // 2).astype(F32)
    s = jnp.einsum('thsk,hsnk->thsn', q, keys.astype(F32))
    sv, si = lax.top_k(s, PEER_TOPK)
    cand = (sv[:, :, 0, :, None] + sv[:, :, 1, None, :]).reshape(T, PEER_HEADS, PEER_TOPK * PEER_TOPK)
    cidx = (si[:, :, 0, :, None] * PEER_KEYS + si[:, :, 1, None, :]).reshape(T, PEER_HEADS, PEER_TOPK * PEER_TOPK)
    top, pos = lax.top_k(cand, PEER_TOPK)
    eidx = jnp.take_along_axis(cidx, pos, axis=-1)
    gate = jax.nn.softmax(top, axis=-1)
    nb = -(-T // PEER_BLOCK)
    pad = nb * PEER_BLOCK - T
    hb = jnp.pad(hf, ((0, pad), (0, 0))).reshape(nb, PEER_BLOCK, D)
    ib = jnp.pad(eidx, ((0, pad), (0, 0), (0, 0))).reshape(nb, PEER_BLOCK, PEER_HEADS, PEER_TOPK)
    gb = jnp.pad(gate, ((0, pad), (0, 0), (0, 0))).reshape(nb, PEER_BLOCK, PEER_HEADS, PEER_TOPK)

    def block(args):
        xb, ixb, gtb = args
        act = jnp.einsum('td,thkd->thk', xb, jnp.take(u_tab, ixb, axis=0)).astype(F32)
        wgt = (gtb * jax.nn.gelu(act)).astype(v_tab.dtype)
        return jnp.einsum('thk,thkd->td', wgt, jnp.take(v_tab, ixb, axis=0))

    out = lax.map(block, (hb, ib, gb))
    return out.reshape(nb * PEER_BLOCK, D)[:T].reshape(b, L, D).astype(h.dtype)


def run_group(x, c, start, st_conv, st_pool, st_sconv, st_ssm, p):
    b = x.shape[0]
    new_conv, new_pool, new_sconv, new_ssm = [], [], [], []
    for l in range(DEPTH):
        pl = {k: v[l] for k, v in p.items()}
        hc = st_conv[l] if st_conv is not None else jnp.zeros((b, CONV_WIDTH - 1, D_CONV), x.dtype)
        hp = st_pool[l] if st_pool is not None else jnp.zeros((b, POOL_HIST, D_POOL), x.dtype)
        hs = st_sconv[l] if st_sconv is not None else jnp.zeros((b, SSM_CONV_WIDTH - 1, SSM_CONV_DIM), x.dtype)
        h0 = st_ssm[l] if st_ssm is not None else jnp.zeros((b, SSM_HEADS, SSM_HEAD_DIM, SSM_STATE), x.dtype)
        ada = jax.nn.silu(c) @ pl['w_ada'] + pl['b_ada']
        sh1, sc1, g1, sh2, sc2, g2 = jnp.split(ada[:, None, :], 6, axis=-1)
        h = layer_norm(x) * (1 + sc1) + sh1
        mix, nc_, np_, ns_, nh_ = token_mixer(h, start, hc, hp, hs, h0, pl)
        x = layer_norm(ALPHA * x + g1 * mix, pl['ln1_g'], pl['ln1_b'])
        h = layer_norm(x) * (1 + sc2) + sh2
        ff = peer(h, pl['peer_wq'], pl['peer_keys'], pl['peer_u'], pl['peer_v'])
        x = layer_norm(ALPHA * x + g2 * ff, pl['ln2_g'], pl['ln2_b'])
        new_conv.append(nc_)
        new_pool.append(np_)
        new_sconv.append(ns_)
        new_ssm.append(nh_)
    return x, jnp.stack(new_conv), jnp.stack(new_pool), jnp.stack(new_sconv), jnp.stack(new_ssm)


def setup_inputs(seed: int = 0) -> dict:
    key = jax.random.key(seed)
    ks = iter(list(jax.random.split(key, 48)))
    D = D_MODEL

    def nrm(shape, scale=1.0):
        return scale * jax.random.normal(next(ks), shape, F32)

    dt0 = jnp.exp(jax.random.uniform(next(ks), (DEPTH, SSM_HEADS), F32, np.log(1e-3), np.log(1e-1)))
    dt_bias = dt0 + jnp.log(-jnp.expm1(-dt0))
    a_log = jnp.log(jax.random.uniform(next(ks), (DEPTH, SSM_HEADS), F32, 1.0, 16.0))
    return {
        'x_prompt': nrm((BATCH, SEQ, D)),
        'x_sample': nrm((DEC_BATCH, DEC_SEQ, D)),
        'state_conv_a': nrm((DEPTH, DEC_BATCH, CONV_WIDTH - 1, D_CONV), 0.5),
        'state_pool': nrm((DEPTH, DEC_BATCH, POOL_HIST, D_POOL)),
        'state_ssm_conv': nrm((DEPTH, DEC_BATCH, SSM_CONV_WIDTH - 1, SSM_CONV_DIM)),
        'state_ssm': nrm((DEPTH, DEC_BATCH, SSM_HEADS, SSM_HEAD_DIM, SSM_STATE), 0.3),
        'c_prompt': nrm((BATCH, D)),
        'c_sample': nrm((DEC_BATCH, D)),
        'w_ada': nrm((DEPTH, D, 6 * D), 0.5 * D ** -0.5),
        'b_ada': nrm((DEPTH, 6 * D), 0.02),
        'w_in': nrm((DEPTH, D, IN_COLS), D ** -0.5),
        'conv_a_w': nrm((DEPTH, CONV_WIDTH, D_CONV), CONV_WIDTH ** -0.5),
        'conv_a_b': nrm((DEPTH, D_CONV), 0.02),
        'ln_a_g': 1.0 + nrm((DEPTH, D_CONV), 0.05),
        'ln_a_b': nrm((DEPTH, D_CONV), 0.02),
        'w_a_out': nrm((DEPTH, D_CONV, D), BETA * D_CONV ** -0.5),
        'pool_w': nrm((DEPTH, POOL_GROUPS, POOL_GROUP_DIM, POOL_GROUP_DIM), POOL_GROUP_DIM ** -0.5),
        'pool_scale': 1.0 + nrm((DEPTH, D_POOL), 0.05),
        'w_b_out': nrm((DEPTH, D_POOL, D), BETA * D_POOL ** -0.5),
        'ssm_conv_w': nrm((DEPTH, SSM_CONV_WIDTH, SSM_CONV_DIM), SSM_CONV_WIDTH ** -0.5),
        'ssm_conv_b': nrm((DEPTH, SSM_CONV_DIM), 0.02),
        'dt_bias': dt_bias,
        'a_log': a_log,
        'd_skip': 1.0 + nrm((DEPTH, SSM_HEADS), 0.1),
        'ssm_norm_g': 1.0 + nrm((DEPTH, D_SSM), 0.05),
        'w_c_out': nrm((DEPTH, D_SSM, D), BETA * D_SSM ** -0.5),
        'w_o': nrm((DEPTH, D, D), BETA * D ** -0.5),
        'ln1_g': 1.0 + nrm((DEPTH, D), 0.05),
        'ln1_b': nrm((DEPTH, D), 0.02),
        'peer_wq': nrm((DEPTH, D, PEER_HEADS * PEER_DK), D ** -0.5),
        'peer_keys': nrm((DEPTH, PEER_HEADS, 2, PEER_KEYS, PEER_DK // 2), (PEER_DK // 2) ** -0.5),
        'peer_u': nrm((DEPTH, PEER_EXPERTS, D), D ** -0.5),
        'peer_v': nrm((DEPTH, PEER_EXPERTS, D), BETA),
        'ln2_g': 1.0 + nrm((DEPTH, D), 0.05),
        'ln2_b': nrm((DEPTH, D), 0.02),
    }


def reference(x_prompt, x_sample, state_conv_a, state_pool, state_ssm_conv, state_ssm, c_prompt, c_sample,
              w_ada, b_ada, w_in, conv_a_w, conv_a_b, ln_a_g, ln_a_b, w_a_out, pool_w, pool_scale, w_b_out,
              ssm_conv_w, ssm_conv_b, dt_bias, a_log, d_skip, ssm_norm_g, w_c_out, w_o, ln1_g, ln1_b,
              peer_wq, peer_keys, peer_u, peer_v, ln2_g, ln2_b):
    p = dict(w_ada=w_ada, b_ada=b_ada, w_in=w_in, conv_a_w=conv_a_w, conv_a_b=conv_a_b, ln_a_g=ln_a_g,
             ln_a_b=ln_a_b, w_a_out=w_a_out, pool_w=pool_w, pool_scale=pool_scale, w_b_out=w_b_out,
             ssm_conv_w=ssm_conv_w, ssm_conv_b=ssm_conv_b, dt_bias=dt_bias, a_log=a_log, d_skip=d_skip,
             ssm_norm_g=ssm_norm_g, w_c_out=w_c_out, w_o=w_o, ln1_g=ln1_g, ln1_b=ln1_b, peer_wq=peer_wq,
             peer_keys=peer_keys, peer_u=peer_u, peer_v=peer_v, ln2_g=ln2_g, ln2_b=ln2_b)
    y_prompt, pc, pp, psc, pss = run_group(x_prompt, c_prompt, 0, None, None, None, None, p)
    y_sample, sc, sp, ssc, sss = run_group(x_sample, c_sample, PAST_LEN, state_conv_a, state_pool,
                                           state_ssm_conv, state_ssm, p)
    return (y_prompt, y_sample, pc, pp, psc, pss, sc, sp, ssc, sss)
```

```python
import functools

import jax
import jax.numpy as jnp
from jax import lax
from jax.experimental import pallas as pl
from jax.experimental.pallas import tpu as pltpu

F32 = jnp.float32
BF16 = jnp.bfloat16
I32 = jnp.int32

LANES = 128
SUBLANES = 8
VMEM_LIMIT_BYTES = 56 * 1024 * 1024

LN_EPS = 1e-5
PAST_LEN = 4096
POOL_WINDOWS = (2, 4, 8, 16)
SSM_HEAD_DIM = 64
SSM_GROUPS = 2
SSM_STATE = 128
SSD_CHUNK = 64
PEER_TOPK = 16
PEER_TOKENS_PER_STEP = 8


def _cparams(*sem):
    return pltpu.CompilerParams(dimension_semantics=sem, vmem_limit_bytes=VMEM_LIMIT_BYTES)


def _ln(x):
    mu = jnp.mean(x, axis=-1, keepdims=True)
    xc = x - mu
    var = jnp.mean(xc * xc, axis=-1, keepdims=True)
    return xc * lax.rsqrt(var + LN_EPS)


def _silu(x):
    return x * jax.nn.sigmoid(x)


def _resident(shape):
    nd = len(shape)
    return pl.BlockSpec(shape, lambda *_: (0,) * nd, pipeline_mode=pl.Buffered(1))


def _ada_kernel(c_ref, w_ref, b_ref, o_ref):
    a = _silu(c_ref[...]).astype(BF16)
    o_ref[...] = jnp.dot(a, w_ref[...].astype(BF16), preferred_element_type=F32) + b_ref[...]


def ada_all_layers(c_all, w_ada, b_ada, tn=1024):
    depth, d, n = w_ada.shape
    rows = c_all.shape[0]
    return pl.pallas_call(
        _ada_kernel,
        grid=(depth, n // tn),
        in_specs=[
            pl.BlockSpec((rows, d), lambda l, j: (0, 0)),
            pl.BlockSpec((None, d, tn), lambda l, j: (l, 0, j)),
            pl.BlockSpec((None, 1, tn), lambda l, j: (l, 0, j)),
        ],
        out_specs=pl.BlockSpec((None, rows, tn), lambda l, j: (l, 0, j)),
        out_shape=jax.ShapeDtypeStruct((depth, rows, n), F32),
        compiler_params=_cparams("arbitrary", "arbitrary"),
        name="ada",
    )(c_all, w_ada, b_ada.reshape(depth, 1, n))


def _mod_spec(mod, tm, tiles_per_group):
    r, d = mod.shape[1], mod.shape[2]
    if r == 1:
        return pl.BlockSpec((None, 1, d), lambda i: (i // tiles_per_group, 0, 0))
    assert r == tm
    return pl.BlockSpec((None, tm, d), lambda i: (i, 0, 0))


def _ln_mod_kernel(x_ref, sh_ref, sc_ref, o_ref):
    o_ref[...] = (_ln(x_ref[...]) * (1.0 + sc_ref[...]) + sh_ref[...]).astype(o_ref.dtype)


def ln_mod(x, shift, scale, tm, tiles_per_group):
    t, d = x.shape
    return pl.pallas_call(
        _ln_mod_kernel,
        grid=(t // tm,),
        in_specs=[pl.BlockSpec((tm, d), lambda i: (i, 0)),
                  _mod_spec(shift, tm, tiles_per_group),
                  _mod_spec(scale, tm, tiles_per_group)],
        out_specs=pl.BlockSpec((tm, d), lambda i: (i, 0)),
        out_shape=jax.ShapeDtypeStruct((t, d), BF16),
        compiler_params=_cparams("arbitrary"),
        name="ln_mod",
    )(x, shift, scale)


def _matmul_kernel(a_ref, w_ref, o_ref):
    o_ref[...] = jnp.dot(a_ref[...], w_ref[...], preferred_element_type=F32).astype(o_ref.dtype)


def matmul(a, w, tm, tn, out_dtype=F32):
    m, k = a.shape
    n = w.shape[1]
    return pl.pallas_call(
        _matmul_kernel,
        grid=(m // tm, n // tn),
        in_specs=[pl.BlockSpec((tm, k), lambda i, j: (i, 0)),
                  pl.BlockSpec((k, tn), lambda i, j: (0, j))],
        out_specs=pl.BlockSpec((tm, tn), lambda i, j: (i, j)),
        out_shape=jax.ShapeDtypeStruct((m, n), out_dtype),
        compiler_params=_cparams("arbitrary", "arbitrary"),
        name="matmul",
    )(a, w)


COL_GATES = 0
COL_A = 6144
COL_Z = 7168
COL_XS = 8192
COL_BC = 9216
COL_P = 9728
COL_DT = 10240
PROJ_COLS = 10368

CONV_ROWS = 32
HIST_A = 32
HIST_P = 16
HIST_S = 8


def _softplus(x):
    return jnp.maximum(x, 0.0) + jnp.log1p(jnp.exp(-jnp.abs(x)))


def _mixer_kernel(a_ref, z_ref, xs_ref, bc_ref, p_ref, dt_ref,
                  ha_ref, hp_ref, hs_ref, h0_ref,
                  caw_ref, cab_ref, lag_ref, lab_ref, pw_ref, psc_ref,
                  scw_ref, scb_ref, dtb_ref, alog_ref, dsk_ref, ng_ref, e_ref, tril_ref,
                  pre_ref, na_ref, np_ref, ns_ref, nh_ref,
                  exa, exq, exs, xcs, dts, ysc, ht,
                  *, tl, q, start, nt):
    t = pl.program_id(1)
    taps_a = caw_ref.shape[0]
    taps_s = scw_ref.shape[0]
    da = cab_ref.shape[1]
    dp = psc_ref.shape[1]
    ds_ = ng_ref.shape[1]
    nbc = SSM_GROUPS * SSM_STATE
    lead_a = HIST_A - (taps_a - 1)
    lead_p = HIST_P - (POOL_WINDOWS[-1] - 1)
    lead_s = HIST_S - (taps_s - 1)

    @pl.when(t == 0)
    def _load_state():
        exa[pl.ds(0, HIST_A), :] = jnp.zeros((HIST_A, da), F32)
        exa[pl.ds(lead_a, taps_a - 1), :] = ha_ref[...]
        exq[pl.ds(0, HIST_P), :] = jnp.zeros((HIST_P, dp), F32)
        exq[pl.ds(lead_p, HIST_P - lead_p), :] = hp_ref[...]
        exs[pl.ds(0, HIST_S), :] = jnp.zeros((HIST_S, ds_ + 2 * nbc), F32)
        exs[pl.ds(lead_s, taps_s - 1), :] = hs_ref[...]
        ht[...] = h0_ref[...].T

    a = a_ref[...]
    exa[pl.ds(HIST_A, tl), :] = a[:, :da] * jax.nn.sigmoid(a[:, da:])
    rows = min(CONV_ROWS, tl)
    for rb in range(tl // rows):
        acc = jnp.zeros((rows, da), F32)
        for k in range(taps_a):
            acc = acc + caw_ref[pl.ds(k, 1), :] * exa[pl.ds(rb * rows + lead_a + k, rows), :]
        ya = _silu(_ln(acc + cab_ref[...]) * lag_ref[...] + lab_ref[...])
        pre_ref[pl.ds(rb * rows, rows), pl.ds(0, da)] = ya.astype(pre_ref.dtype)
    na_ref[...] = exa[pl.ds(tl + lead_a, taps_a - 1), :]
    exa[pl.ds(0, HIST_A), :] = exa[pl.ds(tl, HIST_A), :]

    p = p_ref[...]
    exq[pl.ds(HIST_P, tl), :] = p
    pos = (lax.broadcasted_iota(I32, (tl, 1), 0) + (start + 1) + t * tl).astype(F32)
    gd = dp // len(POOL_WINDOWS)
    for gi, w in enumerate(POOL_WINDOWS):
        cols = pl.ds(gi * gd, gd)
        win = exq[pl.ds(HIST_P, tl), cols]
        for i in range(1, w):
            win = win + exq[pl.ds(HIST_P - i, tl), cols]
        mix = win / jnp.minimum(pos, float(w)) - p[:, gi * gd:(gi + 1) * gd]
        yb = jnp.dot(mix.astype(BF16), pw_ref[gi], preferred_element_type=F32)
        pre_ref[:, pl.ds(da + gi * gd, gd)] = (yb * psc_ref[:, cols]).astype(pre_ref.dtype)
    np_ref[...] = exq[pl.ds(tl + lead_p, HIST_P - lead_p), :]
    exq[pl.ds(0, HIST_P), :] = exq[pl.ds(tl, HIST_P), :]

    exs[pl.ds(HIST_S, tl), pl.ds(0, ds_)] = xs_ref[...]
    exs[pl.ds(HIST_S, tl), pl.ds(ds_, 2 * nbc)] = bc_ref[...]
    for rb in range(tl // rows):
        for cb in range((ds_ + 2 * nbc) // (4 * LANES)):
            cols = pl.ds(cb * 4 * LANES, 4 * LANES)
            acc = jnp.zeros((rows, 4 * LANES), F32)
            for k in range(taps_s):
                acc = acc + scw_ref[pl.ds(k, 1), cols] * exs[pl.ds(rb * rows + lead_s + k, rows), cols]
            xcs[pl.ds(rb * rows, rows), cols] = _silu(acc + scb_ref[:, cols])
    ns_ref[...] = exs[pl.ds(tl + lead_s, taps_s - 1), :]
    exs[pl.ds(0, HIST_S), :] = exs[pl.ds(tl, HIST_S), :]
    dts[...] = _softplus(dt_ref[...] + dtb_ref[...])

    a_neg = -jnp.exp(alog_ref[...])
    causal = lax.broadcasted_iota(I32, (q, q), 0) >= lax.broadcasted_iota(I32, (q, q), 1)
    low_half = lax.broadcasted_iota(I32, (1, LANES), 1) < SSM_HEAD_DIM
    gw = ds_ // SSM_GROUPS
    hi = lax.Precision.HIGHEST

    def chunk(c, carry):
        r0 = pl.multiple_of(c * q, q)
        rws = pl.ds(r0, q)
        dt_c = dts[rws, :]
        acs = jnp.dot(tril_ref[...], dt_c * a_neg, precision=hi, preferred_element_type=F32)
        acs_l = jnp.dot(acs, e_ref[...], precision=hi, preferred_element_type=F32)
        dt_l = jnp.dot(dt_c, e_ref[...], precision=hi, preferred_element_type=F32)
        last_l = acs_l[q - 1:q, :]
        dtx = dt_l * xcs[rws, pl.ds(0, ds_)]
        wend = (jnp.exp(last_l - acs_l) * dtx).astype(BF16)
        dtx_b = dtx.astype(BF16)
        acs_t = acs.T
        for g in range(SSM_GROUPS):
            bg = xcs[rws, pl.ds(ds_ + g * SSM_STATE, SSM_STATE)].astype(BF16)
            cg = xcs[rws, pl.ds(ds_ + nbc + g * SSM_STATE, SSM_STATE)].astype(BF16)
            cbm = lax.dot_general(cg, bg, (((1,), (1,)), ((), ())), preferred_element_type=F32)
            glanes = pl.ds(g * gw, gw)
            for hp in range(gw // LANES):
                lo = g * gw + hp * LANES
                h_even = lo // SSM_HEAD_DIM
                d_pair = dtx_b[:, lo:lo + LANES]
                ys = []
                for hh in (h_even, h_even + 1):
                    seg = acs[:, hh:hh + 1] - acs_t[hh:hh + 1, :]
                    decay = jnp.exp(jnp.where(causal, seg, -jnp.inf))
                    ys.append(jnp.dot((cbm * decay).astype(BF16), d_pair, preferred_element_type=F32))
                ysc[:, pl.ds(lo, LANES)] = jnp.where(low_half, ys[0], ys[1])
            htg = ht[:, glanes]
            st = lax.dot_general(bg, wend[:, g * gw:(g + 1) * gw], (((0,), (0,)), ((), ())),
                                 preferred_element_type=F32)
            yoff = jnp.dot(cg, htg.astype(BF16), preferred_element_type=F32)
            ysc[:, glanes] = ysc[:, glanes] + yoff * jnp.exp(acs_l[:, g * gw:(g + 1) * gw])
            ht[:, glanes] = htg * jnp.exp(last_l[:, g * gw:(g + 1) * gw]) + st
        y = ysc[...] + dsk_ref[...] * xcs[rws, pl.ds(0, ds_)]
        y = y * _silu(z_ref[rws, :])
        y = y * lax.rsqrt(jnp.mean(y * y, axis=-1, keepdims=True) + LN_EPS) * ng_ref[...]
        pre_ref[rws, pl.ds(da + dp, ds_)] = y.astype(pre_ref.dtype)
        return carry

    lax.fori_loop(0, tl // q, chunk, 0)

    @pl.when(t == nt - 1)
    def _store_state():
        nh_ref[...] = ht[...].T


def mixers(proj, hist_a, hist_p, hist_s, h0, prm, batch, seq, tl, start):
    nt = seq // tl
    q = min(SSD_CHUNK, seq)
    da, dp = prm["cab"].shape[1], prm["psc"].shape[1]
    ds_ = prm["ng"].shape[1]
    nbc = SSM_GROUPS * SSM_STATE
    t_all = batch * seq
    heads = ds_ // SSM_HEAD_DIM

    def col(width, off):
        return pl.BlockSpec((tl, width), lambda b, t: (b * nt + t, off // width))

    def per_batch(shape):
        return pl.BlockSpec((None,) + shape, lambda b, t: (b,) + (0,) * len(shape))

    params = [prm[k] for k in ("caw", "cab", "lag", "lab", "pw", "psc", "scw", "scb",
                               "dtb", "alog", "dsk", "ng", "e", "tril")]
    in_specs = [col(2 * da, COL_A), col(ds_, COL_Z), col(ds_, COL_XS), col(2 * nbc, COL_BC),
                col(dp, COL_P), col(LANES, COL_DT),
                per_batch(hist_a.shape[1:]), per_batch(hist_p.shape[1:]),
                per_batch(hist_s.shape[1:]), per_batch((heads * SSM_HEAD_DIM, SSM_STATE))]
    in_specs += [_resident(x.shape) for x in params]
    out_shape = (jax.ShapeDtypeStruct((t_all, da + dp + ds_), BF16),
                 jax.ShapeDtypeStruct(hist_a.shape, F32),
                 jax.ShapeDtypeStruct(hist_p.shape, F32),
                 jax.ShapeDtypeStruct(hist_s.shape, F32),
                 jax.ShapeDtypeStruct((batch, heads * SSM_HEAD_DIM, SSM_STATE), F32))
    out_specs = (pl.BlockSpec((tl, da + dp + ds_), lambda b, t: (b * nt + t, 0)),
                 per_batch(hist_a.shape[1:]), per_batch(hist_p.shape[1:]),
                 per_batch(hist_s.shape[1:]), per_batch((heads * SSM_HEAD_DIM, SSM_STATE)))
    scratch = [pltpu.VMEM((tl + HIST_A, da), F32), pltpu.VMEM((tl + HIST_P, dp), F32),
               pltpu.VMEM((tl + HIST_S, ds_ + 2 * nbc), F32), pltpu.VMEM((tl, ds_ + 2 * nbc), F32),
               pltpu.VMEM((tl, LANES), F32), pltpu.VMEM((q, ds_), F32),
               pltpu.VMEM((SSM_STATE, ds_), F32)]
    pre, na, np_, ns, nh = pl.pallas_call(
        functools.partial(_mixer_kernel, tl=tl, q=q, start=start, nt=nt),
        grid=(batch, nt),
        in_specs=in_specs, out_specs=out_specs, out_shape=out_shape,
        scratch_shapes=scratch,
        compiler_params=_cparams("arbitrary", "arbitrary"),
        name="mixers",
    )(proj, proj, proj, proj, proj, proj, hist_a, hist_p, hist_s,
      h0.reshape(batch, heads * SSM_HEAD_DIM, SSM_STATE), *params)
    return pre, na, np_, ns, nh.reshape(h0.shape)


def _merge_kernel(pre_ref, ga_ref, gb_ref, gc_ref, x_ref, g1_ref, sh_ref, sc_ref,
                  lg_ref, lb_ref, wa_ref, wb_ref, wc_ref, wo_ref,
                  x1_ref, h2_ref, *, alpha):
    da, dp = wa_ref.shape[0], wb_ref.shape[0]
    pre = pre_ref[...]
    mixed = jax.nn.sigmoid(ga_ref[...]) * jnp.dot(pre[:, :da], wa_ref[...], preferred_element_type=F32)
    mixed += jax.nn.sigmoid(gb_ref[...]) * jnp.dot(pre[:, da:da + dp], wb_ref[...], preferred_element_type=F32)
    mixed += jax.nn.sigmoid(gc_ref[...]) * jnp.dot(pre[:, da + dp:], wc_ref[...], preferred_element_type=F32)
    out = jnp.dot(mixed.astype(BF16), wo_ref[...], preferred_element_type=F32)
    x1 = _ln(alpha * x_ref[...] + g1_ref[...] * out) * lg_ref[...] + lb_ref[...]
    x1_ref[...] = x1
    h2_ref[...] = (_ln(x1) * (1.0 + sc_ref[...]) + sh_ref[...]).astype(h2_ref.dtype)


def merge(pre, proj, x, g1, sh2, sc2, ln_g, ln_b, wa, wb, wc, wo, tm, tiles_per_group, alpha):
    t, d = x.shape
    row = lambda w: pl.BlockSpec((tm, w), lambda i: (i, 0))
    gate = lambda k: pl.BlockSpec((tm, d), lambda i: (i, COL_GATES // d + k))
    mods = [_mod_spec(m, tm, tiles_per_group) for m in (g1, sh2, sc2)]
    weights = [ln_g, ln_b, wa, wb, wc, wo]
    return pl.pallas_call(
        functools.partial(_merge_kernel, alpha=alpha),
        grid=(t // tm,),
        in_specs=[row(pre.shape[1]), gate(0), gate(1), gate(2), row(d)] + mods
                 + [_resident(w.shape) for w in weights],
        out_specs=(row(d), row(d)),
        out_shape=(jax.ShapeDtypeStruct((t, d), F32), jax.ShapeDtypeStruct((t, d), BF16)),
        compiler_params=_cparams("arbitrary"),
        name="merge",
    )(pre, proj, proj, proj, x, g1, sh2, sc2, *weights)


def _top_rows(s, payload, k):
    r = s.shape[0]
    pos = lax.broadcasted_iota(I32, s.shape, 0)
    vals, pays = [], []
    for _ in range(k):
        m = jnp.max(s, axis=0, keepdims=True)
        first = jnp.min(jnp.where(s == m, pos, r), axis=0, keepdims=True)
        hit = pos == first
        vals.append(m)
        pays.append(first if payload is None else jnp.max(jnp.where(hit, payload, -1), axis=0, keepdims=True))
        s = jnp.where(hit, -jnp.inf, s)
    return jnp.concatenate(vals, axis=0), jnp.concatenate(pays, axis=0)


def _route_kernel(q_ref, k_ref, eid_ref, gate_ref):
    nkeys, dk = k_ref.shape[1], k_ref.shape[2]
    heads = k_ref.shape[0] // 2
    kk = PEER_TOPK
    for h in range(heads):
        side = []
        for s in range(2):
            qs = q_ref[:, pl.ds((2 * h + s) * dk, dk)].astype(BF16)
            sc = lax.dot_general(k_ref[2 * h + s], qs, (((1,), (1,)), ((), ())), preferred_element_type=F32)
            side.append(_top_rows(sc, None, kk))
        (v1, i1), (v2, i2) = side
        cand = jnp.concatenate([v2 + v1[i:i + 1, :] for i in range(kk)], axis=0)
        cidx = jnp.concatenate([i2 + i1[i:i + 1, :] * nkeys for i in range(kk)], axis=0)
        top, eid = _top_rows(cand, cidx, kk)
        e = jnp.exp(top - top[0:1, :])
        gate_ref[pl.ds(h * kk, kk), :] = e / jnp.sum(e, axis=0, keepdims=True)
        eid_ref[pl.ds(h * kk, kk), :] = eid


ROW_SUBLANES = 32
ROW_GROUP = 8


def _ln_tile(x):
    n = x.shape[0] * x.shape[1]
    mu = jnp.sum(jnp.sum(x, axis=1, keepdims=True), axis=0, keepdims=True) / n
    xc = x - mu
    var = jnp.sum(jnp.sum(xc * xc, axis=1, keepdims=True), axis=0, keepdims=True) / n
    return xc * lax.rsqrt(var + LN_EPS)


def _peer_kernel(idc_ref, idn_ref, x1_ref, gate_ref, sh_ref, sc_ref, g2_ref, lg_ref, lb_ref, uv_ref,
                 o_ref, buf, sem, *, alpha, nsteps, tb, kk):
    i = pl.program_id(0)
    slot = lax.rem(i, 2)
    rows_per_step = tb * kk
    half = ROW_SUBLANES // 2

    def row_copy(ids_ref, s, r):
        e = ids_ref[0, r]
        src = uv_ref.at[pl.ds(pl.multiple_of(e * ROW_SUBLANES, ROW_SUBLANES), ROW_SUBLANES), :]
        dst = buf.at[s, pl.ds(pl.multiple_of(r * ROW_SUBLANES, ROW_SUBLANES), ROW_SUBLANES), :]
        return pltpu.make_async_copy(src, dst, sem.at[s])

    def issue(ids_ref, s):
        def body(r8, c):
            for u in range(ROW_GROUP):
                row_copy(ids_ref, s, r8 * ROW_GROUP + u).start()
            return c
        lax.fori_loop(0, rows_per_step // ROW_GROUP, body, 0)

    @pl.when(i == 0)
    def _first():
        issue(idc_ref, 0)

    @pl.when(i + 1 < nsteps)
    def _prefetch():
        issue(idn_ref, 1 - slot)

    pltpu.make_async_copy(uv_ref.at[pl.ds(0, rows_per_step * ROW_SUBLANES), :], buf.at[slot], sem.at[slot]).wait()

    for t in range(tb):
        x1 = x1_ref[t]
        h = _ln_tile(x1) * (1.0 + sc_ref[t % sc_ref.shape[0]]) + sh_ref[t % sh_ref.shape[0]]

        def group(j, acc):
            base = pl.multiple_of((t * kk + j * ROW_GROUP) * ROW_SUBLANES, ROW_GROUP * ROW_SUBLANES)
            blk = buf[slot, pl.ds(base, ROW_GROUP * ROW_SUBLANES), :].reshape(ROW_GROUP, ROW_SUBLANES, LANES)
            act = jnp.sum(jnp.sum(blk[:, :half, :] * h[None], axis=1), axis=-1, keepdims=True)
            g = gate_ref[pl.ds(pl.multiple_of(j * ROW_GROUP, ROW_GROUP), ROW_GROUP), t:t + 1]
            c = g * jax.nn.gelu(act)
            return acc + jnp.sum(c[:, :, None] * blk[:, half:, :], axis=0)

        ff = lax.fori_loop(0, kk // ROW_GROUP, group, jnp.zeros(x1.shape, F32))
        y = alpha * x1 + g2_ref[t % g2_ref.shape[0]] * ff
        o_ref[t] = _ln_tile(y) * lg_ref[0] + lb_ref[0]


def peer(x1, eid_t, gate_t, sh2, sc2, g2, ln_g, ln_b, uv, steps_per_group, alpha):
    t, s, _ = x1.shape
    kk = eid_t.shape[0]
    tb = PEER_TOKENS_PER_STEP
    nsteps = t // tb
    ids = eid_t.T.reshape(nsteps, 1, tb * kk)
    gates = gate_t.reshape(kk, nsteps, tb).transpose(1, 0, 2)

    def mod_spec(m):
        if m.shape[1] == 1:
            return pl.BlockSpec((None, 1, s, LANES), lambda i: (i // steps_per_group, 0, 0, 0))
        return pl.BlockSpec((None, tb, s, LANES), lambda i: (i, 0, 0, 0))

    tile = pl.BlockSpec((tb, s, LANES), lambda i: (i, 0, 0))
    return pl.pallas_call(
        functools.partial(_peer_kernel, alpha=alpha, nsteps=nsteps, tb=tb, kk=kk),
        grid=(nsteps,),
        in_specs=[pl.BlockSpec((None, 1, tb * kk), lambda i: (i, 0, 0), memory_space=pltpu.SMEM),
                  pl.BlockSpec((None, 1, tb * kk), lambda i: (jnp.minimum(i + 1, nsteps - 1), 0, 0),
                               memory_space=pltpu.SMEM),
                  tile,
                  pl.BlockSpec((None, kk, tb), lambda i: (i, 0, 0)),
                  mod_spec(sh2), mod_spec(sc2), mod_spec(g2),
                  _resident(ln_g.shape), _resident(ln_b.shape),
                  pl.BlockSpec(memory_space=pl.ANY)],
        out_specs=tile,
        out_shape=jax.ShapeDtypeStruct(x1.shape, F32),
        scratch_shapes=[pltpu.VMEM((2, tb * kk * ROW_SUBLANES, LANES), F32), pltpu.SemaphoreType.DMA((2,))],
        compiler_params=_cparams("arbitrary"),
        name="peer",
    )(ids, ids, x1, gates, sh2, sc2, g2, ln_g, ln_b, uv)


def route(q, keys, tm=LANES):
    t = q.shape[0]
    rows = keys.shape[0] // 2 * PEER_TOPK
    return pl.pallas_call(
        _route_kernel,
        grid=(t // tm,),
        in_specs=[pl.BlockSpec((tm, q.shape[1]), lambda i: (i, 0)), _resident(keys.shape)],
        out_specs=(pl.BlockSpec((rows, tm), lambda i: (0, i)), pl.BlockSpec((rows, tm), lambda i: (0, i))),
        out_shape=(jax.ShapeDtypeStruct((rows, t), I32), jax.ShapeDtypeStruct((rows, t), F32)),
        compiler_params=_cparams("arbitrary"),
        name="route",
    )(q, keys)


def _tile_sizes(batch, seq):
    if seq >= 1024:
        return dict(ln=512, proj=1024, mix=256, merge=256, wq=1024, per_row_mods=False)
    return dict(ln=128, proj=batch * seq, mix=seq, merge=128, wq=batch * seq, per_row_mods=True)


def _run_layer(x, ada_rows, states, w, batch, seq, start, alpha):
    rows, d = x.shape
    ts = _tile_sizes(batch, seq)
    vecs = jnp.split(ada_rows, 6, axis=-1)

    def mods(v, tm):
        if ts["per_row_mods"]:
            return jnp.repeat(v, seq, axis=0).reshape(rows // tm, tm, d)
        return v.reshape(batch, 1, d)

    def tile_mods(v):
        s = d // LANES
        if ts["per_row_mods"]:
            return jnp.repeat(v, seq, axis=0).reshape(rows // PEER_TOKENS_PER_STEP, PEER_TOKENS_PER_STEP, s, LANES)
        return v.reshape(batch, 1, s, LANES)

    h = ln_mod(x, mods(vecs[0], ts["ln"]), mods(vecs[1], ts["ln"]), ts["ln"], seq // ts["ln"] if seq >= ts["ln"] else 1)
    proj = matmul(h, w["w_in"], ts["proj"], PROJ_COLS // 9)
    pre, na, npool, ns, nh = mixers(proj, *states, w["mix"], batch, seq, ts["mix"], start)
    tm = ts["merge"]
    x1, h2 = merge(pre, proj, x, mods(vecs[2], tm), mods(vecs[3], tm), mods(vecs[4], tm), w["ln1_g"], w["ln1_b"],
                   w["w_a_out"], w["w_b_out"], w["w_c_out"], w["w_o"], tm, seq // tm if seq >= tm else 1, alpha)
    q = matmul(h2, w["wq"], ts["wq"], w["wq"].shape[1] // 2)
    eid_t, gate_t = route(q, w["keys"])
    s = d // LANES
    xo = peer(x1.reshape(rows, s, LANES), eid_t, gate_t, tile_mods(vecs[3]), tile_mods(vecs[4]), tile_mods(vecs[5]),
              w["ln2_g"].reshape(1, s, LANES), w["ln2_b"].reshape(1, s, LANES), w["uv"],
              seq // PEER_TOKENS_PER_STEP if not ts["per_row_mods"] else 1, alpha)
    return xo.reshape(rows, d), (na, npool, ns, nh)


def kernel(x_prompt, x_sample, state_conv_a, state_pool, state_ssm_conv, state_ssm, c_prompt, c_sample, w_ada, b_ada, w_in, conv_a_w, conv_a_b, ln_a_g, ln_a_b, w_a_out, pool_w, pool_scale, w_b_out, ssm_conv_w, ssm_conv_b, dt_bias, a_log, d_skip, ssm_norm_g, w_c_out, w_o, ln1_g, ln1_b, peer_wq, peer_keys, peer_u, peer_v, ln2_g, ln2_b):
    depth, d, _ = w_in.shape
    alpha = (2 * depth) ** 0.25
    bp, lp, _ = x_prompt.shape
    bs, ls, _ = x_sample.shape
    da, dpool, dssm = w_a_out.shape[1], w_b_out.shape[1], w_c_out.shape[1]
    nbc = SSM_GROUPS * SSM_STATE
    heads = dt_bias.shape[1]
    assert (da, dpool, dssm, d) == (512, 512, 1024, 2048) and heads * SSM_HEAD_DIM == dssm
    assert COL_A == 3 * d and PROJ_COLS == COL_DT + LANES

    n_c = bp + bs
    c_all = jnp.pad(jnp.concatenate([c_prompt, c_sample], axis=0), ((0, -n_c % SUBLANES), (0, 0)))
    ada = ada_all_layers(c_all, w_ada, b_ada)

    o_p, o_z, o_x = 2 * da, 2 * da + dpool, 2 * da + dpool + dssm
    o_bc, o_dt = o_x + dssm, o_x + dssm + 2 * nbc
    o_g = o_dt + heads

    def permute_in(wl):
        return jnp.concatenate([wl[:, o_g:], wl[:, :2 * da], wl[:, o_z:o_x], wl[:, o_x:o_bc], wl[:, o_bc:o_dt],
                                wl[:, o_p:o_z], wl[:, o_dt:o_g], jnp.zeros((d, LANES - heads), wl.dtype)], axis=1)

    lane_pad = lambda v: jnp.pad(v, (0, LANES - heads))[None]
    expand = (jnp.arange(dssm)[None, :] // SSM_HEAD_DIM == jnp.arange(LANES)[:, None]).astype(F32)

    def layer_weights(l, q):
        mix = dict(caw=conv_a_w[l], cab=conv_a_b[l][None], lag=ln_a_g[l][None], lab=ln_a_b[l][None],
                   pw=pool_w[l].astype(BF16), psc=pool_scale[l][None], scw=ssm_conv_w[l], scb=ssm_conv_b[l][None],
                   dtb=lane_pad(dt_bias[l]), alog=lane_pad(a_log[l]), dsk=jnp.repeat(d_skip[l], SSM_HEAD_DIM)[None],
                   ng=ssm_norm_g[l][None], e=expand, tril=jnp.tril(jnp.ones((q, q), F32)))
        return mix

    x_p = x_prompt.reshape(bp * lp, d)
    x_s = x_sample.reshape(bs * ls, d)
    zeros_p = (jnp.zeros((bp,) + state_conv_a.shape[2:], F32), jnp.zeros((bp,) + state_pool.shape[2:], F32),
               jnp.zeros((bp,) + state_ssm_conv.shape[2:], F32), jnp.zeros((bp,) + state_ssm.shape[2:], F32))
    new_p, new_s = [], []
    for l in range(depth):
        w = dict(w_in=permute_in(w_in[l]).astype(BF16), w_a_out=w_a_out[l].astype(BF16),
                 w_b_out=w_b_out[l].astype(BF16), w_c_out=w_c_out[l].astype(BF16), w_o=w_o[l].astype(BF16),
                 wq=peer_wq[l].astype(BF16), keys=peer_keys[l].reshape((-1,) + peer_keys.shape[3:]).astype(BF16),
                 uv=jnp.concatenate([peer_u[l], peer_v[l]], axis=-1).reshape(-1, LANES),
                 ln1_g=ln1_g[l][None], ln1_b=ln1_b[l][None], ln2_g=ln2_g[l], ln2_b=ln2_b[l])
        w["mix"] = layer_weights(l, min(SSD_CHUNK, lp))
        x_p, st = _run_layer(x_p, ada[l, :bp], zeros_p, w, bp, lp, 0, alpha)
        new_p.append(st)
        w["mix"] = layer_weights(l, min(SSD_CHUNK, ls))
        x_s, st = _run_layer(x_s, ada[l, bp:n_c], (state_conv_a[l], state_pool[l], state_ssm_conv[l], state_ssm[l]),
                             w, bs, ls, PAST_LEN, alpha)
        new_s.append(st)
    stack = lambda sts, k: jnp.stack([s[k] for s in sts])
    return (x_p.reshape(x_prompt.shape), x_s.reshape(x_sample.shape),
            stack(new_p, 0), stack(new_p, 1), stack(new_p, 2), stack(new_p, 3),
            stack(new_s, 0), stack(new_s, 1), stack(new_s, 2), stack(new_s, 3))
```

```python
import functools

import jax
import jax.numpy as jnp
from jax import lax
from jax.experimental import pallas as pl
from jax.experimental.pallas import tpu as pltpu

F32 = jnp.float32
BF16 = jnp.bfloat16
I32 = jnp.int32

LANES = 128
SUBLANES = 8
VMEM_LIMIT_BYTES = 56 * 1024 * 1024

LN_EPS = 1e-5
PAST_LEN = 4096
POOL_WINDOWS = (2, 4, 8, 16)
SSM_HEAD_DIM = 64
SSM_GROUPS = 2
SSM_STATE = 128
SSD_CHUNK = 64
PEER_TOPK = 16
PEER_TOKENS_PER_STEP = 8


def _cparams(*sem):
    return pltpu.CompilerParams(dimension_semantics=sem, vmem_limit_bytes=VMEM_LIMIT_BYTES)


def _ln(x):
    mu = jnp.mean(x, axis=-1, keepdims=True)
    xc = x - mu
    var = jnp.mean(xc * xc, axis=-1, keepdims=True)
    return xc * lax.rsqrt(var + LN_EPS)


def _silu(x):
    return x * jax.nn.sigmoid(x)


def _resident(shape):
    nd = len(shape)
    return pl.BlockSpec(shape, lambda *_: (0,) * nd, pipeline_mode=pl.Buffered(1))


def _ada_kernel(c_ref, w_ref, b_ref, o_ref):
    a = _silu(c_ref[...]).astype(BF16)
    o_ref[...] = jnp.dot(a, w_ref[...].astype(BF16), preferred_element_type=F32) + b_ref[...]


def ada_all_layers(c_all, w_ada, b_ada, tn=1024):
    depth, d, n = w_ada.shape
    rows = c_all.shape[0]
    return pl.pallas_call(
        _ada_kernel,
        grid=(depth, n // tn),
        in_specs=[
            pl.BlockSpec((rows, d), lambda l, j: (0, 0)),
            pl.BlockSpec((None, d, tn), lambda l, j: (l, 0, j)),
            pl.BlockSpec((None, 1, tn), lambda l, j: (l, 0, j)),
        ],
        out_specs=pl.BlockSpec((None, rows, tn), lambda l, j: (l, 0, j)),
        out_shape=jax.ShapeDtypeStruct((depth, rows, n), F32),
        compiler_params=_cparams("arbitrary", "arbitrary"),
        name="ada",
    )(c_all, w_ada, b_ada.reshape(depth, 1, n))


def _mod_spec(mod, tm, tiles_per_group):
    r, d = mod.shape[1], mod.shape[2]
    if r == 1:
        return pl.BlockSpec((None, 1, d), lambda i: (i // tiles_per_group, 0, 0))
    assert r == tm
    return pl.BlockSpec((None, tm, d), lambda i: (i, 0, 0))


def _rows_from_tiles(xt_ref, tm):
    s = xt_ref.shape[0] // tm
    return jnp.concatenate([xt_ref[pl.ds(k, tm, stride=s), :] for k in range(s)], axis=1)


def _rows_to_tiles(xt_ref, x):
    tm = x.shape[0]
    s = xt_ref.shape[0] // tm
    for k in range(s):
        xt_ref[pl.ds(k, tm, stride=s), :] = x[:, k * LANES:(k + 1) * LANES]


def _ln_mod_kernel(xt_ref, sh_ref, sc_ref, o_ref):
    x = _rows_from_tiles(xt_ref, o_ref.shape[0])
    o_ref[...] = (_ln(x) * (1.0 + sc_ref[...]) + sh_ref[...]).astype(o_ref.dtype)


def ln_mod(xt, shift, scale, tm, tiles_per_group):
    d = shift.shape[2]
    s = d // LANES
    t = xt.shape[0] // s
    return pl.pallas_call(
        _ln_mod_kernel,
        grid=(t // tm,),
        in_specs=[pl.BlockSpec((tm * s, LANES), lambda i: (i, 0)),
                  _mod_spec(shift, tm, tiles_per_group),
                  _mod_spec(scale, tm, tiles_per_group)],
        out_specs=pl.BlockSpec((tm, d), lambda i: (i, 0)),
        out_shape=jax.ShapeDtypeStruct((t, d), BF16),
        compiler_params=_cparams("arbitrary"),
        name="ln_mod",
    )(xt, shift, scale)


def _matmul_kernel(a_ref, w_ref, o_ref):
    o_ref[...] = jnp.dot(a_ref[...], w_ref[...], preferred_element_type=F32).astype(o_ref.dtype)


def matmul(a, w, tm, tn, out_dtype=F32):
    m, k = a.shape
    n = w.shape[1]
    return pl.pallas_call(
        _matmul_kernel,
        grid=(m // tm, n // tn),
        in_specs=[pl.BlockSpec((tm, k), lambda i, j: (i, 0)),
                  pl.BlockSpec((k, tn), lambda i, j: (0, j))],
        out_specs=pl.BlockSpec((tm, tn), lambda i, j: (i, j)),
        out_shape=jax.ShapeDtypeStruct((m, n), out_dtype),
        compiler_params=_cparams("arbitrary", "arbitrary"),
        name="matmul",
    )(a, w)


COL_GATES = 0
COL_A = 6144
COL_Z = 7168
COL_XS = 8192
COL_BC = 9216
COL_P = 9728
COL_DT = 10240
PROJ_COLS = 10368

CONV_ROWS = 32
HIST_A = 32
HIST_P = 16
HIST_S = 8


def _softplus(x):
    return jnp.maximum(x, 0.0) + jnp.log1p(jnp.exp(-jnp.abs(x)))


def _mixer_kernel(a_ref, z_ref, xs_ref, bc_ref, p_ref, dt_ref,
                  ha_ref, hp_ref, hs_ref, h0_ref,
                  caw_ref, cab_ref, lag_ref, lab_ref, pw_ref, psc_ref,
                  scw_ref, scb_ref, dtb_ref, alog_ref, dsk_ref, ng_ref, e_ref, tril_ref,
                  pre_ref, na_ref, np_ref, ns_ref, nh_ref,
                  exa, exq, exs, xcs, dts, ysc, ht,
                  *, tl, q, start, nt):
    t = pl.program_id(1)
    taps_a = caw_ref.shape[0]
    taps_s = scw_ref.shape[0]
    da = cab_ref.shape[1]
    dp = psc_ref.shape[1]
    ds_ = ng_ref.shape[1]
    nbc = SSM_GROUPS * SSM_STATE
    lead_a = HIST_A - (taps_a - 1)
    lead_p = HIST_P - (POOL_WINDOWS[-1] - 1)
    lead_s = HIST_S - (taps_s - 1)

    @pl.when(t == 0)
    def _load_state():
        exa[pl.ds(0, HIST_A), :] = jnp.zeros((HIST_A, da), F32)
        exa[pl.ds(lead_a, taps_a - 1), :] = ha_ref[...]
        exq[pl.ds(0, HIST_P), :] = jnp.zeros((HIST_P, dp), F32)
        exq[pl.ds(lead_p, HIST_P - lead_p), :] = hp_ref[...]
        exs[pl.ds(0, HIST_S), :] = jnp.zeros((HIST_S, ds_ + 2 * nbc), F32)
        exs[pl.ds(lead_s, taps_s - 1), :] = hs_ref[...]
        ht[...] = h0_ref[...].T

    a = a_ref[...]
    exa[pl.ds(HIST_A, tl), :] = a[:, :da] * jax.nn.sigmoid(a[:, da:])
    rows = min(CONV_ROWS, tl)
    for rb in range(tl // rows):
        acc = jnp.zeros((rows, da), F32)
        for k in range(taps_a):
            acc = acc + caw_ref[pl.ds(k, 1), :] * exa[pl.ds(rb * rows + lead_a + k, rows), :]
        ya = _silu(_ln(acc + cab_ref[...]) * lag_ref[...] + lab_ref[...])
        pre_ref[pl.ds(rb * rows, rows), pl.ds(0, da)] = ya.astype(pre_ref.dtype)
    na_ref[...] = exa[pl.ds(tl + lead_a, taps_a - 1), :]
    exa[pl.ds(0, HIST_A), :] = exa[pl.ds(tl, HIST_A), :]

    p = p_ref[...]
    exq[pl.ds(HIST_P, tl), :] = p
    pos = (lax.broadcasted_iota(I32, (tl, 1), 0) + (start + 1) + t * tl).astype(F32)
    gd = dp // len(POOL_WINDOWS)
    for gi, w in enumerate(POOL_WINDOWS):
        cols = pl.ds(gi * gd, gd)
        win = exq[pl.ds(HIST_P, tl), cols]
        for i in range(1, w):
            win = win + exq[pl.ds(HIST_P - i, tl), cols]
        mix = win / jnp.minimum(pos, float(w)) - p[:, gi * gd:(gi + 1) * gd]
        yb = jnp.dot(mix.astype(BF16), pw_ref[gi], preferred_element_type=F32)
        pre_ref[:, pl.ds(da + gi * gd, gd)] = (yb * psc_ref[:, cols]).astype(pre_ref.dtype)
    np_ref[...] = exq[pl.ds(tl + lead_p, HIST_P - lead_p), :]
    exq[pl.ds(0, HIST_P), :] = exq[pl.ds(tl, HIST_P), :]

    exs[pl.ds(HIST_S, tl), pl.ds(0, ds_)] = xs_ref[...]
    exs[pl.ds(HIST_S, tl), pl.ds(ds_, 2 * nbc)] = bc_ref[...]
    for rb in range(tl // rows):
        for cb in range((ds_ + 2 * nbc) // (4 * LANES)):
            cols = pl.ds(cb * 4 * LANES, 4 * LANES)
            acc = jnp.zeros((rows, 4 * LANES), F32)
            for k in range(taps_s):
                acc = acc + scw_ref[pl.ds(k, 1), cols] * exs[pl.ds(rb * rows + lead_s + k, rows), cols]
            xcs[pl.ds(rb * rows, rows), cols] = _silu(acc + scb_ref[:, cols])
    ns_ref[...] = exs[pl.ds(tl + lead_s, taps_s - 1), :]
    exs[pl.ds(0, HIST_S), :] = exs[pl.ds(tl, HIST_S), :]
    dts[...] = _softplus(dt_ref[...] + dtb_ref[...])

    a_neg = -jnp.exp(alog_ref[...])
    causal = lax.broadcasted_iota(I32, (q, q), 0) >= lax.broadcasted_iota(I32, (q, q), 1)
    low_half = lax.broadcasted_iota(I32, (1, LANES), 1) < SSM_HEAD_DIM
    gw = ds_ // SSM_GROUPS
    hi = lax.Precision.HIGHEST

    def chunk(c, carry):
        r0 = pl.multiple_of(c * q, q)
        rws = pl.ds(r0, q)
        dt_c = dts[rws, :]
        acs = jnp.dot(tril_ref[...], dt_c * a_neg, precision=hi, preferred_element_type=F32)
        acs_l = jnp.dot(acs, e_ref[...], precision=hi, preferred_element_type=F32)
        dt_l = jnp.dot(dt_c, e_ref[...], precision=hi, preferred_element_type=F32)
        last_l = acs_l[q - 1:q, :]
        dtx = dt_l * xcs[rws, pl.ds(0, ds_)]
        wend = (jnp.exp(last_l - acs_l) * dtx).astype(BF16)
        dtx_b = dtx.astype(BF16)
        acs_t = acs.T
        for g in range(SSM_GROUPS):
            bg = xcs[rws, pl.ds(ds_ + g * SSM_STATE, SSM_STATE)].astype(BF16)
            cg = xcs[rws, pl.ds(ds_ + nbc + g * SSM_STATE, SSM_STATE)].astype(BF16)
            cbm = lax.dot_general(cg, bg, (((1,), (1,)), ((), ())), preferred_element_type=F32)
            glanes = pl.ds(g * gw, gw)
            for hp in range(gw // LANES):
                lo = g * gw + hp * LANES
                h_even = lo // SSM_HEAD_DIM
                d_pair = dtx_b[:, lo:lo + LANES]
                ys = []
                for hh in (h_even, h_even + 1):
                    seg = acs[:, hh:hh + 1] - acs_t[hh:hh + 1, :]
                    decay = jnp.exp(jnp.where(causal, seg, -jnp.inf))
                    ys.append(jnp.dot((cbm * decay).astype(BF16), d_pair, preferred_element_type=F32))
                ysc[:, pl.ds(lo, LANES)] = jnp.where(low_half, ys[0], ys[1])
            htg = ht[:, glanes]
            st = lax.dot_general(bg, wend[:, g * gw:(g + 1) * gw], (((0,), (0,)), ((), ())),
                                 preferred_element_type=F32)
            yoff = jnp.dot(cg, htg.astype(BF16), preferred_element_type=F32)
            ysc[:, glanes] = ysc[:, glanes] + yoff * jnp.exp(acs_l[:, g * gw:(g + 1) * gw])
            ht[:, glanes] = htg * jnp.exp(last_l[:, g * gw:(g + 1) * gw]) + st
        y = ysc[...] + dsk_ref[...] * xcs[rws, pl.ds(0, ds_)]
        y = y * _silu(z_ref[rws, :])
        y = y * lax.rsqrt(jnp.mean(y * y, axis=-1, keepdims=True) + LN_EPS) * ng_ref[...]
        pre_ref[rws, pl.ds(da + dp, ds_)] = y.astype(pre_ref.dtype)
        return carry

    lax.fori_loop(0, tl // q, chunk, 0)

    @pl.when(t == nt - 1)
    def _store_state():
        nh_ref[...] = ht[...].T


def mixers(proj, hist_a, hist_p, hist_s, h0, prm, batch, seq, tl, start):
    nt = seq // tl
    q = min(SSD_CHUNK, seq)
    da, dp = prm["cab"].shape[1], prm["psc"].shape[1]
    ds_ = prm["ng"].shape[1]
    nbc = SSM_GROUPS * SSM_STATE
    t_all = batch * seq
    heads = ds_ // SSM_HEAD_DIM

    def col(width, off):
        return pl.BlockSpec((tl, width), lambda b, t: (b * nt + t, off // width))

    def per_batch(shape):
        return pl.BlockSpec((None,) + shape, lambda b, t: (b,) + (0,) * len(shape))

    params = [prm[k] for k in ("caw", "cab", "lag", "lab", "pw", "psc", "scw", "scb",
                               "dtb", "alog", "dsk", "ng", "e", "tril")]
    in_specs = [col(2 * da, COL_A), col(ds_, COL_Z), col(ds_, COL_XS), col(2 * nbc, COL_BC),
                col(dp, COL_P), col(LANES, COL_DT),
                per_batch(hist_a.shape[1:]), per_batch(hist_p.shape[1:]),
                per_batch(hist_s.shape[1:]), per_batch((heads * SSM_HEAD_DIM, SSM_STATE))]
    in_specs += [_resident(x.shape) for x in params]
    out_shape = (jax.ShapeDtypeStruct((t_all, da + dp + ds_), BF16),
                 jax.ShapeDtypeStruct(hist_a.shape, F32),
                 jax.ShapeDtypeStruct(hist_p.shape, F32),
                 jax.ShapeDtypeStruct(hist_s.shape, F32),
                 jax.ShapeDtypeStruct((batch, heads * SSM_HEAD_DIM, SSM_STATE), F32))
    out_specs = (pl.BlockSpec((tl, da + dp + ds_), lambda b, t: (b * nt + t, 0)),
                 per_batch(hist_a.shape[1:]), per_batch(hist_p.shape[1:]),
                 per_batch(hist_s.shape[1:]), per_batch((heads * SSM_HEAD_DIM, SSM_STATE)))
    scratch = [pltpu.VMEM((tl + HIST_A, da), F32), pltpu.VMEM((tl + HIST_P, dp), F32),
               pltpu.VMEM((tl + HIST_S, ds_ + 2 * nbc), F32), pltpu.VMEM((tl, ds_ + 2 * nbc), F32),
               pltpu.VMEM((tl, LANES), F32), pltpu.VMEM((q, ds_), F32),
               pltpu.VMEM((SSM_STATE, ds_), F32)]
    pre, na, np_, ns, nh = pl.pallas_call(
        functools.partial(_mixer_kernel, tl=tl, q=q, start=start, nt=nt),
        grid=(batch, nt),
        in_specs=in_specs, out_specs=out_specs, out_shape=out_shape,
        scratch_shapes=scratch,
        compiler_params=_cparams("arbitrary", "arbitrary"),
        name="mixers",
    )(proj, proj, proj, proj, proj, proj, hist_a, hist_p, hist_s,
      h0.reshape(batch, heads * SSM_HEAD_DIM, SSM_STATE), *params)
    return pre, na, np_, ns, nh.reshape(h0.shape)


def _merge_kernel(pre_ref, ga_ref, gb_ref, gc_ref, x_ref, g1_ref, sh_ref, sc_ref,
                  lg_ref, lb_ref, wa_ref, wb_ref, wc_ref, wo_ref,
                  x1_ref, h2_ref, *, alpha):
    da, dp = wa_ref.shape[0], wb_ref.shape[0]
    pre = pre_ref[...]
    mixed = jax.nn.sigmoid(ga_ref[...]) * jnp.dot(pre[:, :da], wa_ref[...], preferred_element_type=F32)
    mixed += jax.nn.sigmoid(gb_ref[...]) * jnp.dot(pre[:, da:da + dp], wb_ref[...], preferred_element_type=F32)
    mixed += jax.nn.sigmoid(gc_ref[...]) * jnp.dot(pre[:, da + dp:], wc_ref[...], preferred_element_type=F32)
    out = jnp.dot(mixed.astype(BF16), wo_ref[...], preferred_element_type=F32)
    x = _rows_from_tiles(x_ref, pre.shape[0])
    x1 = _ln(alpha * x + g1_ref[...] * out) * lg_ref[...] + lb_ref[...]
    _rows_to_tiles(x1_ref, x1)
    h2_ref[...] = (_ln(x1) * (1.0 + sc_ref[...]) + sh_ref[...]).astype(h2_ref.dtype)


def merge(pre, proj, xt, g1, sh2, sc2, ln_g, ln_b, wa, wb, wc, wo, tm, tiles_per_group, alpha):
    t = pre.shape[0]
    d = wo.shape[1]
    s = d // LANES
    row = lambda w: pl.BlockSpec((tm, w), lambda i: (i, 0))
    tiles = pl.BlockSpec((tm * s, LANES), lambda i: (i, 0))
    gate = lambda k: pl.BlockSpec((tm, d), lambda i: (i, COL_GATES // d + k))
    mods = [_mod_spec(m, tm, tiles_per_group) for m in (g1, sh2, sc2)]
    weights = [ln_g, ln_b, wa, wb, wc, wo]
    return pl.pallas_call(
        functools.partial(_merge_kernel, alpha=alpha),
        grid=(t // tm,),
        in_specs=[row(pre.shape[1]), gate(0), gate(1), gate(2), tiles] + mods
                 + [_resident(w.shape) for w in weights],
        out_specs=(tiles, row(d)),
        out_shape=(jax.ShapeDtypeStruct((t * s, LANES), F32), jax.ShapeDtypeStruct((t, d), BF16)),
        compiler_params=_cparams("arbitrary"),
        name="merge",
    )(pre, proj, proj, proj, xt, g1, sh2, sc2, *weights)


def _pack_kernel(u_ref, v_ref, o_ref):
    r, d = u_ref.shape
    s = d // LANES
    for k in range(s):
        o_ref[pl.ds(k, r, stride=2 * s), :] = u_ref[:, k * LANES:(k + 1) * LANES]
        o_ref[pl.ds(s + k, r, stride=2 * s), :] = v_ref[:, k * LANES:(k + 1) * LANES]


def pack_tables(u, v, layer, rows=256):
    _, e, d = u.shape
    per = 2 * d // LANES
    tab = pl.BlockSpec((None, rows, d), lambda i: (layer, i, 0))
    return pl.pallas_call(
        _pack_kernel,
        grid=(e // rows,),
        in_specs=[tab, tab],
        out_specs=pl.BlockSpec((rows * per, LANES), lambda i: (i, 0)),
        out_shape=jax.ShapeDtypeStruct((e * per, LANES), F32),
        compiler_params=_cparams("arbitrary"),
        name="pack_tables",
    )(u, v)


def _top_rows(s, pos, payload, k):
    big = jnp.iinfo(jnp.int32).max
    vals, pays = [], []
    for _ in range(k):
        m = jnp.max(s, axis=0, keepdims=True)
        first = jnp.min(jnp.where(s == m, pos, big), axis=0, keepdims=True)
        hit = pos == first
        vals.append(m)
        pays.append(first if payload is None else jnp.max(jnp.where(hit, payload, -1), axis=0, keepdims=True))
        s = jnp.where(hit, -jnp.inf, s)
    return jnp.concatenate(vals, axis=0), jnp.concatenate(pays, axis=0)


def _pair_candidates(kk):
    blocks = [(i, kk // (i + 1)) for i in range(kk) if kk // (i + 1) > 1]
    tail = [i for i in range(kk) if kk // (i + 1) == 1]
    return blocks, tail


def _route_kernel(q_ref, k_ref, eid_ref, gate_ref):
    nkeys, dk = k_ref.shape[1], k_ref.shape[2]
    heads = k_ref.shape[0] // 2
    kk = PEER_TOPK
    tm = q_ref.shape[0]
    key_pos = lax.broadcasted_iota(I32, (nkeys, tm), 0)
    blocks, tail = _pair_candidates(kk)
    assert tail == list(range(tail[0], kk)) and len(tail) == SUBLANES and tail[0] % SUBLANES == 0
    j_iota = lax.broadcasted_iota(I32, (kk, tm), 0)
    cpos = jnp.concatenate([j_iota + i * kk for i, _ in blocks]
                           + [(lax.broadcasted_iota(I32, (len(tail), tm), 0) + tail[0]) * kk], axis=0)
    for h in range(heads):
        side = []
        for s in range(2):
            qs = q_ref[:, pl.ds((2 * h + s) * dk, dk)].astype(BF16)
            sc = lax.dot_general(k_ref[2 * h + s], qs, (((1,), (1,)), ((), ())), preferred_element_type=F32)
            side.append(_top_rows(sc, key_pos, None, kk))
        (v1, i1), (v2, i2) = side
        cand = jnp.concatenate([jnp.where(j_iota < cnt, v2 + v1[i:i + 1, :], -jnp.inf) for i, cnt in blocks]
                               + [v1[tail[0]:, :] + v2[0:1, :]], axis=0)
        cidx = jnp.concatenate([i2 + i1[i:i + 1, :] * nkeys for i, _ in blocks]
                               + [i1[tail[0]:, :] * nkeys + i2[0:1, :]], axis=0)
        top, eid = _top_rows(cand, cpos, cidx, kk)
        e = jnp.exp(top - top[0:1, :])
        gate_ref[pl.ds(h * kk, kk), :] = e / jnp.sum(e, axis=0, keepdims=True)
        eid_ref[pl.ds(h * kk, kk), :] = eid


ROW_SUBLANES = 32
ROW_GROUP = 8


def _ln_tiles(x):
    n = x.shape[1] * x.shape[2]
    mu = jnp.sum(jnp.sum(x, axis=2, keepdims=True), axis=1, keepdims=True) / n
    xc = x - mu
    var = jnp.sum(jnp.sum(xc * xc, axis=2, keepdims=True), axis=1, keepdims=True) / n
    return xc * lax.rsqrt(var + LN_EPS)


def _sublane_sums(ms, roll, where, sub):
    def stage(x, y, dist, off):
        keep_x = ((sub - off) % (2 * dist)) < dist if dist == 4 else ((sub - off - dist) % (2 * dist)) < dist
        return where(keep_x, x, y) + roll(where(keep_x, y, x), dist)

    halves = [stage(ms[2 * k], ms[2 * k + 1], 4, off) for k, off in enumerate((0, 2, 1, 3))]
    quarters = [stage(halves[0], halves[1], 2, 0), stage(halves[2], halves[3], 2, 1)]
    return stage(quarters[0], quarters[1], 1, 0)


def _sublane_sum_order():
    import numpy as np
    sub = np.arange(SUBLANES)[:, None]
    ms = [np.eye(SUBLANES)[k][None, :].repeat(SUBLANES, 0) for k in range(SUBLANES)]
    out = _sublane_sums(ms, lambda x, s: np.roll(x, s, axis=0), np.where, sub)
    assert np.array_equal(np.sort(out, axis=1)[:, :-1], np.zeros((SUBLANES, SUBLANES - 1)))
    assert np.array_equal(out.max(axis=1), np.full(SUBLANES, float(SUBLANES)))
    order = out.argmax(axis=1)
    assert sorted(order) == list(range(SUBLANES))
    return [int(k) for k in order]


def _peer_kernel(idc_ref, idn_ref, x1_ref, gate_ref, sh_ref, sc_ref, g2_ref, lg_ref, lb_ref, uv_ref,
                 o_ref, buf, sem, hs, rs, cs, ffs, *, alpha, nsteps, tb, kk):
    i = pl.program_id(0)
    slot = lax.rem(i, 2)
    rows_per_step = tb * kk
    half = ROW_SUBLANES // 2
    s_tok = x1_ref.shape[1]
    assert s_tok == half and half == 2 * SUBLANES and kk % ROW_GROUP == 0 and ROW_GROUP == SUBLANES

    def row_copy(ids_ref, s, r):
        e = ids_ref[0, r]
        src = uv_ref.at[pl.ds(pl.multiple_of(e * ROW_SUBLANES, ROW_SUBLANES), ROW_SUBLANES), :]
        dst = buf.at[s, pl.ds(pl.multiple_of(r * ROW_SUBLANES, ROW_SUBLANES), ROW_SUBLANES), :]
        return pltpu.make_async_copy(src, dst, sem.at[s])

    groups = kk // ROW_GROUP

    @pl.when(i == 0)
    def _first():
        def body(n, c):
            for k in range(ROW_GROUP):
                row_copy(idc_ref, 0, n * ROW_GROUP + k).start()
            return c
        lax.fori_loop(0, tb * groups, body, 0)

    def wait_slot(s):
        pltpu.make_async_copy(uv_ref.at[pl.ds(0, rows_per_step * ROW_SUBLANES), :], buf.at[s], sem.at[s]).wait()

    wait_slot(slot)

    hs[...] = (_ln_tiles(x1_ref[...]) * (1.0 + sc_ref[...]) + sh_ref[...]).reshape(tb * s_tok, LANES)

    sub = lax.broadcasted_iota(I32, (SUBLANES, LANES), 0)
    order = _sublane_sum_order()
    slot_of = [order.index(i) for i in range(SUBLANES)]
    roll = lambda x, s: pltpu.roll(x, s, 0)

    def dots(n, c):
        for k in range(ROW_GROUP):
            row_copy(idn_ref, 1 - slot, n * ROW_GROUP + k).start()
        t = n // groups
        h0 = hs[pl.ds(pl.multiple_of(t * s_tok, s_tok), SUBLANES), :]
        h1 = hs[pl.ds(pl.multiple_of(t * s_tok + SUBLANES, SUBLANES), SUBLANES), :]
        base = pl.multiple_of(n * (ROW_GROUP * ROW_SUBLANES), ROW_GROUP * ROW_SUBLANES)
        ms = []
        for k in range(ROW_GROUP):
            u0 = buf[slot, pl.ds(base + k * ROW_SUBLANES, SUBLANES), :]
            u1 = buf[slot, pl.ds(base + k * ROW_SUBLANES + SUBLANES, SUBLANES), :]
            ms.append(u0 * h0 + u1 * h1)
        rs[pl.ds(pl.multiple_of(n * ROW_GROUP, ROW_GROUP), ROW_GROUP), :] = _sublane_sums(
            [ms[slot_of[i]] for i in range(ROW_GROUP)], roll, jnp.where, sub)
        return c

    lax.fori_loop(0, tb * groups, dots, 0)

    act = jnp.sum(rs[...], axis=1, keepdims=True)
    tok_id = lax.broadcasted_iota(I32, (tb, LANES), 0)
    for t in range(tb):
        gcol = lax.dot_general(gate_ref[...], (tok_id == t).astype(F32), (((0,), (0,)), ((), ())),
                               precision=lax.Precision.HIGHEST, preferred_element_type=F32)
        cs[pl.ds(t * kk, kk), :] = gcol * jax.nn.gelu(act[t * kk:(t + 1) * kk])

    def token(t, c):
        def axpy(j, acc):
            n = t * groups + j
            base = pl.multiple_of(n * (ROW_GROUP * ROW_SUBLANES), ROW_GROUP * ROW_SUBLANES)
            cg = cs[pl.ds(pl.multiple_of(n * ROW_GROUP, ROW_GROUP), ROW_GROUP), :]
            acc = list(acc)
            for k in range(ROW_GROUP):
                ck = cg[k:k + 1, :]
                v0 = buf[slot, pl.ds(base + k * ROW_SUBLANES + half, SUBLANES), :]
                v1 = buf[slot, pl.ds(base + k * ROW_SUBLANES + half + SUBLANES, SUBLANES), :]
                acc[2 * (k % 2)] = acc[2 * (k % 2)] + ck * v0
                acc[2 * (k % 2) + 1] = acc[2 * (k % 2) + 1] + ck * v1
            return tuple(acc)

        zero = jnp.zeros((SUBLANES, LANES), F32)
        a0, a1, b0, b1 = lax.fori_loop(0, groups, axpy, (zero, zero, zero, zero))
        r0 = pl.multiple_of(t * s_tok, s_tok)
        ffs[pl.ds(r0, SUBLANES), :] = a0 + b0
        ffs[pl.ds(r0 + SUBLANES, SUBLANES), :] = a1 + b1
        return c

    lax.fori_loop(0, tb, token, 0)

    y = alpha * x1_ref[...] + g2_ref[...] * ffs[...].reshape(tb, s_tok, LANES)
    o_ref[...] = _ln_tiles(y) * lg_ref[...] + lb_ref[...]

    @pl.when(i == nsteps - 1)
    def _drain():
        wait_slot(1 - slot)


def peer(x1, eid_t, gate_t, sh2, sc2, g2, ln_g, ln_b, uv, steps_per_group, alpha):
    t, s, _ = x1.shape
    kk = eid_t.shape[0]
    tb = PEER_TOKENS_PER_STEP
    nsteps = t // tb
    ids = eid_t.T.reshape(nsteps, 1, tb * kk)
    gates = gate_t.T.reshape(nsteps, tb, kk)

    def mod_spec(m):
        if m.shape[1] == 1:
            return pl.BlockSpec((None, 1, s, LANES), lambda i: (i // steps_per_group, 0, 0, 0))
        return pl.BlockSpec((None, tb, s, LANES), lambda i: (i, 0, 0, 0))

    tile = pl.BlockSpec((tb, s, LANES), lambda i: (i, 0, 0))
    return pl.pallas_call(
        functools.partial(_peer_kernel, alpha=alpha, nsteps=nsteps, tb=tb, kk=kk),
        grid=(nsteps,),
        in_specs=[pl.BlockSpec((None, 1, tb * kk), lambda i: (i, 0, 0), memory_space=pltpu.SMEM),
                  pl.BlockSpec((None, 1, tb * kk), lambda i: (jnp.minimum(i + 1, nsteps - 1), 0, 0),
                               memory_space=pltpu.SMEM),
                  tile,
                  pl.BlockSpec((None, tb, kk), lambda i: (i, 0, 0)),
                  mod_spec(sh2), mod_spec(sc2), mod_spec(g2),
                  _resident(ln_g.shape), _resident(ln_b.shape),
                  pl.BlockSpec(memory_space=pl.ANY)],
        out_specs=tile,
        out_shape=jax.ShapeDtypeStruct(x1.shape, F32),
        scratch_shapes=[pltpu.VMEM((2, tb * kk * ROW_SUBLANES, LANES), F32), pltpu.SemaphoreType.DMA((2,)),
                        pltpu.VMEM((tb * s, LANES), F32), pltpu.VMEM((tb * kk, LANES), F32),
                        pltpu.VMEM((tb * kk, LANES), F32), pltpu.VMEM((tb * s, LANES), F32)],
        compiler_params=_cparams("arbitrary"),
        name="peer",
    )(ids, ids, x1, gates, sh2, sc2, g2, ln_g, ln_b, uv)


def route(q, keys, tm=LANES):
    t = q.shape[0]
    rows = keys.shape[0] // 2 * PEER_TOPK
    return pl.pallas_call(
        _route_kernel,
        grid=(t // tm,),
        in_specs=[pl.BlockSpec((tm, q.shape[1]), lambda i: (i, 0)), _resident(keys.shape)],
        out_specs=(pl.BlockSpec((rows, tm), lambda i: (0, i)), pl.BlockSpec((rows, tm), lambda i: (0, i))),
        out_shape=(jax.ShapeDtypeStruct((rows, t), I32), jax.ShapeDtypeStruct((rows, t), F32)),
        compiler_params=_cparams("arbitrary"),
        name="route",
    )(q, keys)


def _tile_sizes(batch, seq):
    if seq >= 1024:
        return dict(ln=512, proj=1024, mix=256, merge=256, wq=1024, per_row_mods=False)
    return dict(ln=128, proj=batch * seq, mix=seq, merge=128, wq=batch * seq, per_row_mods=True)


def _run_layer(x, ada_rows, states, w, batch, seq, start, alpha):
    rows = batch * seq
    d = ada_rows.shape[1] // 6
    ts = _tile_sizes(batch, seq)
    vecs = jnp.split(ada_rows, 6, axis=-1)

    def mods(v, tm):
        if ts["per_row_mods"]:
            return jnp.repeat(v, seq, axis=0).reshape(rows // tm, tm, d)
        return v.reshape(batch, 1, d)

    def tile_mods(v):
        s = d // LANES
        if ts["per_row_mods"]:
            return jnp.repeat(v, seq, axis=0).reshape(rows // PEER_TOKENS_PER_STEP, PEER_TOKENS_PER_STEP, s, LANES)
        return v.reshape(batch, 1, s, LANES)

    h = ln_mod(x, mods(vecs[0], ts["ln"]), mods(vecs[1], ts["ln"]), ts["ln"], seq // ts["ln"] if seq >= ts["ln"] else 1)
    proj = matmul(h, w["w_in"], ts["proj"], PROJ_COLS // 9)
    pre, na, npool, ns, nh = mixers(proj, *states, w["mix"], batch, seq, ts["mix"], start)
    tm = ts["merge"]
    x1, h2 = merge(pre, proj, x, mods(vecs[2], tm), mods(vecs[3], tm), mods(vecs[4], tm), w["ln1_g"], w["ln1_b"],
                   w["w_a_out"], w["w_b_out"], w["w_c_out"], w["w_o"], tm, seq // tm if seq >= tm else 1, alpha)
    q = matmul(h2, w["wq"], ts["wq"], w["wq"].shape[1] // 2)
    eid_t, gate_t = route(q, w["keys"])
    s = d // LANES
    xo = peer(x1.reshape(rows, s, LANES), eid_t, gate_t, tile_mods(vecs[3]), tile_mods(vecs[4]), tile_mods(vecs[5]),
              w["ln2_g"].reshape(1, s, LANES), w["ln2_b"].reshape(1, s, LANES), w["uv"],
              seq // PEER_TOKENS_PER_STEP if not ts["per_row_mods"] else 1, alpha)
    return xo.reshape(rows * s, LANES), (na, npool, ns, nh)


def kernel(x_prompt, x_sample, state_conv_a, state_pool, state_ssm_conv, state_ssm, c_prompt, c_sample, w_ada, b_ada, w_in, conv_a_w, conv_a_b, ln_a_g, ln_a_b, w_a_out, pool_w, pool_scale, w_b_out, ssm_conv_w, ssm_conv_b, dt_bias, a_log, d_skip, ssm_norm_g, w_c_out, w_o, ln1_g, ln1_b, peer_wq, peer_keys, peer_u, peer_v, ln2_g, ln2_b):
    depth, d, _ = w_in.shape
    alpha = (2 * depth) ** 0.25
    bp, lp, _ = x_prompt.shape
    bs, ls, _ = x_sample.shape
    da, dpool, dssm = w_a_out.shape[1], w_b_out.shape[1], w_c_out.shape[1]
    nbc = SSM_GROUPS * SSM_STATE
    heads = dt_bias.shape[1]
    assert (da, dpool, dssm, d) == (512, 512, 1024, 2048) and heads * SSM_HEAD_DIM == dssm
    assert COL_A == 3 * d and PROJ_COLS == COL_DT + LANES

    n_c = bp + bs
    c_all = jnp.pad(jnp.concatenate([c_prompt, c_sample], axis=0), ((0, -n_c % SUBLANES), (0, 0)))
    ada = ada_all_layers(c_all, w_ada, b_ada)

    o_p, o_z, o_x = 2 * da, 2 * da + dpool, 2 * da + dpool + dssm
    o_bc, o_dt = o_x + dssm, o_x + dssm + 2 * nbc
    o_g = o_dt + heads

    def permute_in(wl):
        return jnp.concatenate([wl[:, o_g:], wl[:, :2 * da], wl[:, o_z:o_x], wl[:, o_x:o_bc], wl[:, o_bc:o_dt],
                                wl[:, o_p:o_z], wl[:, o_dt:o_g], jnp.zeros((d, LANES - heads), wl.dtype)], axis=1)

    lane_pad = lambda v: jnp.pad(v, (0, LANES - heads))[None]
    expand = (jnp.arange(dssm)[None, :] // SSM_HEAD_DIM == jnp.arange(LANES)[:, None]).astype(F32)

    def layer_weights(l, q):
        mix = dict(caw=conv_a_w[l], cab=conv_a_b[l][None], lag=ln_a_g[l][None], lab=ln_a_b[l][None],
                   pw=pool_w[l].astype(BF16), psc=pool_scale[l][None], scw=ssm_conv_w[l], scb=ssm_conv_b[l][None],
                   dtb=lane_pad(dt_bias[l]), alog=lane_pad(a_log[l]), dsk=jnp.repeat(d_skip[l], SSM_HEAD_DIM)[None],
                   ng=ssm_norm_g[l][None], e=expand, tril=jnp.tril(jnp.ones((q, q), F32)))
        return mix

    x_p = x_prompt.reshape(-1, LANES)
    x_s = x_sample.reshape(-1, LANES)
    zeros_p = (jnp.zeros((bp,) + state_conv_a.shape[2:], F32), jnp.zeros((bp,) + state_pool.shape[2:], F32),
               jnp.zeros((bp,) + state_ssm_conv.shape[2:], F32), jnp.zeros((bp,) + state_ssm.shape[2:], F32))
    new_p, new_s = [], []
    for l in range(depth):
        w = dict(w_in=permute_in(w_in[l]).astype(BF16), w_a_out=w_a_out[l].astype(BF16),
                 w_b_out=w_b_out[l].astype(BF16), w_c_out=w_c_out[l].astype(BF16), w_o=w_o[l].astype(BF16),
                 wq=peer_wq[l].astype(BF16), keys=peer_keys[l].reshape((-1,) + peer_keys.shape[3:]).astype(BF16),
                 uv=pack_tables(peer_u, peer_v, l),
                 ln1_g=ln1_g[l][None], ln1_b=ln1_b[l][None], ln2_g=ln2_g[l], ln2_b=ln2_b[l])
        w["mix"] = layer_weights(l, min(SSD_CHUNK, lp))
        x_p, st = _run_layer(x_p, ada[l, :bp], zeros_p, w, bp, lp, 0, alpha)
        new_p.append(st)
        w["mix"] = layer_weights(l, min(SSD_CHUNK, ls))
        x_s, st = _run_layer(x_s, ada[l, bp:n_c], (state_conv_a[l], state_pool[l], state_ssm_conv[l], state_ssm[l]),
                             w, bs, ls, PAST_LEN, alpha)
        new_s.append(st)
    stack = lambda sts, k: jnp.stack([s[k] for s in sts])
    return (x_p.reshape(x_prompt.shape), x_s.reshape(x_sample.shape),
            stack(new_p, 0), stack(new_p, 1), stack(new_p, 2), stack(new_p, 3),
            stack(new_s, 0), stack(new_s, 1), stack(new_s, 2), stack(new_s, 3))
```

```python
import functools

import jax
import jax.numpy as jnp
from jax import lax
from jax.experimental import pallas as pl
from jax.experimental.pallas import tpu as pltpu

F32 = jnp.float32
BF16 = jnp.bfloat16
I32 = jnp.int32

LANES = 128
SUBLANES = 8
VMEM_LIMIT_BYTES = 56 * 1024 * 1024

LN_EPS = 1e-5
PAST_LEN = 4096
POOL_WINDOWS = (2, 4, 8, 16)
SSM_HEAD_DIM = 64
SSM_GROUPS = 2
SSM_STATE = 128
SSD_CHUNK = 64
PEER_TOPK = 16
PEER_TOKENS_PER_STEP = 8


def _cparams(*sem):
    return pltpu.CompilerParams(dimension_semantics=sem, vmem_limit_bytes=VMEM_LIMIT_BYTES)


def _ln(x):
    mu = jnp.mean(x, axis=-1, keepdims=True)
    xc = x - mu
    var = jnp.mean(xc * xc, axis=-1, keepdims=True)
    return xc * lax.rsqrt(var + LN_EPS)


def _silu(x):
    return x * jax.nn.sigmoid(x)


def _resident(shape):
    nd = len(shape)
    return pl.BlockSpec(shape, lambda *_: (0,) * nd, pipeline_mode=pl.Buffered(1))


def _ada_kernel(c_ref, w_ref, b_ref, o_ref):
    a = _silu(c_ref[...]).astype(BF16)
    o_ref[...] = jnp.dot(a, w_ref[...].astype(BF16), preferred_element_type=F32) + b_ref[...]


def ada_all_layers(c_all, w_ada, b_ada, tn=1024):
    depth, d, n = w_ada.shape
    rows = c_all.shape[0]
    return pl.pallas_call(
        _ada_kernel,
        grid=(depth, n // tn),
        in_specs=[
            pl.BlockSpec((rows, d), lambda l, j: (0, 0)),
            pl.BlockSpec((None, d, tn), lambda l, j: (l, 0, j)),
            pl.BlockSpec((None, 1, tn), lambda l, j: (l, 0, j)),
        ],
        out_specs=pl.BlockSpec((None, rows, tn), lambda l, j: (l, 0, j)),
        out_shape=jax.ShapeDtypeStruct((depth, rows, n), F32),
        compiler_params=_cparams("arbitrary", "arbitrary"),
        name="ada",
    )(c_all, w_ada, b_ada.reshape(depth, 1, n))


def _mod_spec(mod, tm, tiles_per_group):
    r, d = mod.shape[1], mod.shape[2]
    if r == 1:
        return pl.BlockSpec((None, 1, d), lambda i: (i // tiles_per_group, 0, 0))
    assert r == tm
    return pl.BlockSpec((None, tm, d), lambda i: (i, 0, 0))


def _rows_from_tiles(xt_ref, tm):
    s = xt_ref.shape[0] // tm
    return jnp.concatenate([xt_ref[pl.ds(k, tm, stride=s), :] for k in range(s)], axis=1)


def _rows_to_tiles(xt_ref, x):
    tm = x.shape[0]
    s = xt_ref.shape[0] // tm
    for k in range(s):
        xt_ref[pl.ds(k, tm, stride=s), :] = x[:, k * LANES:(k + 1) * LANES]


def _ln_mod_kernel(xt_ref, sh_ref, sc_ref, o_ref):
    x = _rows_from_tiles(xt_ref, o_ref.shape[0])
    o_ref[...] = (_ln(x) * (1.0 + sc_ref[...]) + sh_ref[...]).astype(o_ref.dtype)


def ln_mod(xt, shift, scale, tm, tiles_per_group):
    d = shift.shape[2]
    s = d // LANES
    t = xt.shape[0] // s
    return pl.pallas_call(
        _ln_mod_kernel,
        grid=(t // tm,),
        in_specs=[pl.BlockSpec((tm * s, LANES), lambda i: (i, 0)),
                  _mod_spec(shift, tm, tiles_per_group),
                  _mod_spec(scale, tm, tiles_per_group)],
        out_specs=pl.BlockSpec((tm, d), lambda i: (i, 0)),
        out_shape=jax.ShapeDtypeStruct((t, d), BF16),
        compiler_params=_cparams("arbitrary"),
        name="ln_mod",
    )(xt, shift, scale)


def _matmul_kernel(a_ref, w_ref, o_ref):
    o_ref[...] = jnp.dot(a_ref[...], w_ref[...], preferred_element_type=F32).astype(o_ref.dtype)


def matmul(a, w, tm, tn, out_dtype=F32):
    m, k = a.shape
    n = w.shape[1]
    return pl.pallas_call(
        _matmul_kernel,
        grid=(m // tm, n // tn),
        in_specs=[pl.BlockSpec((tm, k), lambda i, j: (i, 0)),
                  pl.BlockSpec((k, tn), lambda i, j: (0, j))],
        out_specs=pl.BlockSpec((tm, tn), lambda i, j: (i, j)),
        out_shape=jax.ShapeDtypeStruct((m, n), out_dtype),
        compiler_params=_cparams("arbitrary", "arbitrary"),
        name="matmul",
    )(a, w)


COL_GATES = 0
COL_A = 6144
COL_Z = 7168
COL_XS = 8192
COL_BC = 9216
COL_P = 9728
COL_DT = 10240
PROJ_COLS = 10368

CONV_ROWS = 32
HIST_A = 32
HIST_P = 16
HIST_S = 8


def _softplus(x):
    return jnp.maximum(x, 0.0) + jnp.log1p(jnp.exp(-jnp.abs(x)))


def _mixer_kernel(a_ref, z_ref, xs_ref, bc_ref, p_ref, dt_ref,
                  ha_ref, hp_ref, hs_ref, h0_ref,
                  caw_ref, cab_ref, lag_ref, lab_ref, pw_ref, psc_ref,
                  scw_ref, scb_ref, dtb_ref, alog_ref, dsk_ref, ng_ref, e_ref, tril_ref,
                  pre_ref, na_ref, np_ref, ns_ref, nh_ref,
                  exa, exq, exs, xcs, dts, ysc, ht,
                  *, tl, q, start, nt):
    t = pl.program_id(1)
    taps_a = caw_ref.shape[0]
    taps_s = scw_ref.shape[0]
    da = cab_ref.shape[1]
    dp = psc_ref.shape[1]
    ds_ = ng_ref.shape[1]
    nbc = SSM_GROUPS * SSM_STATE
    lead_a = HIST_A - (taps_a - 1)
    lead_p = HIST_P - (POOL_WINDOWS[-1] - 1)
    lead_s = HIST_S - (taps_s - 1)

    @pl.when(t == 0)
    def _load_state():
        exa[pl.ds(0, HIST_A), :] = jnp.zeros((HIST_A, da), F32)
        exa[pl.ds(lead_a, taps_a - 1), :] = ha_ref[...]
        exq[pl.ds(0, HIST_P), :] = jnp.zeros((HIST_P, dp), F32)
        exq[pl.ds(lead_p, HIST_P - lead_p), :] = hp_ref[...]
        exs[pl.ds(0, HIST_S), :] = jnp.zeros((HIST_S, ds_ + 2 * nbc), F32)
        exs[pl.ds(lead_s, taps_s - 1), :] = hs_ref[...]
        ht[...] = h0_ref[...].T

    a = a_ref[...]
    exa[pl.ds(HIST_A, tl), :] = a[:, :da] * jax.nn.sigmoid(a[:, da:])
    rows = min(CONV_ROWS, tl)
    for rb in range(tl // rows):
        acc = jnp.zeros((rows, da), F32)
        for k in range(taps_a):
            acc = acc + caw_ref[pl.ds(k, 1), :] * exa[pl.ds(rb * rows + lead_a + k, rows), :]
        ya = _silu(_ln(acc + cab_ref[...]) * lag_ref[...] + lab_ref[...])
        pre_ref[pl.ds(rb * rows, rows), pl.ds(0, da)] = ya.astype(pre_ref.dtype)
    na_ref[...] = exa[pl.ds(tl + lead_a, taps_a - 1), :]
    exa[pl.ds(0, HIST_A), :] = exa[pl.ds(tl, HIST_A), :]

    p = p_ref[...]
    exq[pl.ds(HIST_P, tl), :] = p
    pos = (lax.broadcasted_iota(I32, (tl, 1), 0) + (start + 1) + t * tl).astype(F32)
    gd = dp // len(POOL_WINDOWS)
    for gi, w in enumerate(POOL_WINDOWS):
        cols = pl.ds(gi * gd, gd)
        win = exq[pl.ds(HIST_P, tl), cols]
        for i in range(1, w):
            win = win + exq[pl.ds(HIST_P - i, tl), cols]
        mix = win / jnp.minimum(pos, float(w)) - p[:, gi * gd:(gi + 1) * gd]
        yb = jnp.dot(mix.astype(BF16), pw_ref[gi], preferred_element_type=F32)
        pre_ref[:, pl.ds(da + gi * gd, gd)] = (yb * psc_ref[:, cols]).astype(pre_ref.dtype)
    np_ref[...] = exq[pl.ds(tl + lead_p, HIST_P - lead_p), :]
    exq[pl.ds(0, HIST_P), :] = exq[pl.ds(tl, HIST_P), :]

    exs[pl.ds(HIST_S, tl), pl.ds(0, ds_)] = xs_ref[...]
    exs[pl.ds(HIST_S, tl), pl.ds(ds_, 2 * nbc)] = bc_ref[...]
    for rb in range(tl // rows):
        for cb in range((ds_ + 2 * nbc) // (4 * LANES)):
            cols = pl.ds(cb * 4 * LANES, 4 * LANES)
            acc = jnp.zeros((rows, 4 * LANES), F32)
            for k in range(taps_s):
                acc = acc + scw_ref[pl.ds(k, 1), cols] * exs[pl.ds(rb * rows + lead_s + k, rows), cols]
            xcs[pl.ds(rb * rows, rows), cols] = _silu(acc + scb_ref[:, cols])
    ns_ref[...] = exs[pl.ds(tl + lead_s, taps_s - 1), :]
    exs[pl.ds(0, HIST_S), :] = exs[pl.ds(tl, HIST_S), :]
    dts[...] = _softplus(dt_ref[...] + dtb_ref[...])

    a_neg = -jnp.exp(alog_ref[...])
    causal = lax.broadcasted_iota(I32, (q, q), 0) >= lax.broadcasted_iota(I32, (q, q), 1)
    low_half = lax.broadcasted_iota(I32, (1, LANES), 1) < SSM_HEAD_DIM
    gw = ds_ // SSM_GROUPS
    hi = lax.Precision.HIGHEST

    def chunk(c, carry):
        r0 = pl.multiple_of(c * q, q)
        rws = pl.ds(r0, q)
        dt_c = dts[rws, :]
        acs = jnp.dot(tril_ref[...], dt_c * a_neg, precision=hi, preferred_element_type=F32)
        acs_l = jnp.dot(acs, e_ref[...], precision=hi, preferred_element_type=F32)
        dt_l = jnp.dot(dt_c, e_ref[...], precision=hi, preferred_element_type=F32)
        last_l = acs_l[q - 1:q, :]
        dtx = dt_l * xcs[rws, pl.ds(0, ds_)]
        wend = (jnp.exp(last_l - acs_l) * dtx).astype(BF16)
        dtx_b = dtx.astype(BF16)
        acs_t = acs.T
        for g in range(SSM_GROUPS):
            bg = xcs[rws, pl.ds(ds_ + g * SSM_STATE, SSM_STATE)].astype(BF16)
            cg = xcs[rws, pl.ds(ds_ + nbc + g * SSM_STATE, SSM_STATE)].astype(BF16)
            cbm = lax.dot_general(cg, bg, (((1,), (1,)), ((), ())), preferred_element_type=F32)
            glanes = pl.ds(g * gw, gw)
            for hp in range(gw // LANES):
                lo = g * gw + hp * LANES
                h_even = lo // SSM_HEAD_DIM
                d_pair = dtx_b[:, lo:lo + LANES]
                ys = []
                for hh in (h_even, h_even + 1):
                    seg = acs[:, hh:hh + 1] - acs_t[hh:hh + 1, :]
                    decay = jnp.exp(jnp.where(causal, seg, -jnp.inf))
                    ys.append(jnp.dot((cbm * decay).astype(BF16), d_pair, preferred_element_type=F32))
                ysc[:, pl.ds(lo, LANES)] = jnp.where(low_half, ys[0], ys[1])
            htg = ht[:, glanes]
            st = lax.dot_general(bg, wend[:, g * gw:(g + 1) * gw], (((0,), (0,)), ((), ())),
                                 preferred_element_type=F32)
            yoff = jnp.dot(cg, htg.astype(BF16), preferred_element_type=F32)
            ysc[:, glanes] = ysc[:, glanes] + yoff * jnp.exp(acs_l[:, g * gw:(g + 1) * gw])
            ht[:, glanes] = htg * jnp.exp(last_l[:, g * gw:(g + 1) * gw]) + st
        y = ysc[...] + dsk_ref[...] * xcs[rws, pl.ds(0, ds_)]
        y = y * _silu(z_ref[rws, :])
        y = y * lax.rsqrt(jnp.mean(y * y, axis=-1, keepdims=True) + LN_EPS) * ng_ref[...]
        pre_ref[rws, pl.ds(da + dp, ds_)] = y.astype(pre_ref.dtype)
        return carry

    lax.fori_loop(0, tl // q, chunk, 0)

    @pl.when(t == nt - 1)
    def _store_state():
        nh_ref[...] = ht[...].T


def mixers(proj, hist_a, hist_p, hist_s, h0, prm, batch, seq, tl, start):
    nt = seq // tl
    q = min(SSD_CHUNK, seq)
    da, dp = prm["cab"].shape[1], prm["psc"].shape[1]
    ds_ = prm["ng"].shape[1]
    nbc = SSM_GROUPS * SSM_STATE
    t_all = batch * seq
    heads = ds_ // SSM_HEAD_DIM

    def col(width, off):
        return pl.BlockSpec((tl, width), lambda b, t: (b * nt + t, off // width))

    def per_batch(shape):
        return pl.BlockSpec((None,) + shape, lambda b, t: (b,) + (0,) * len(shape))

    params = [prm[k] for k in ("caw", "cab", "lag", "lab", "pw", "psc", "scw", "scb",
                               "dtb", "alog", "dsk", "ng", "e", "tril")]
    in_specs = [col(2 * da, COL_A), col(ds_, COL_Z), col(ds_, COL_XS), col(2 * nbc, COL_BC),
                col(dp, COL_P), col(LANES, COL_DT),
                per_batch(hist_a.shape[1:]), per_batch(hist_p.shape[1:]),
                per_batch(hist_s.shape[1:]), per_batch((heads * SSM_HEAD_DIM, SSM_STATE))]
    in_specs += [_resident(x.shape) for x in params]
    out_shape = (jax.ShapeDtypeStruct((t_all, da + dp + ds_), BF16),
                 jax.ShapeDtypeStruct(hist_a.shape, F32),
                 jax.ShapeDtypeStruct(hist_p.shape, F32),
                 jax.ShapeDtypeStruct(hist_s.shape, F32),
                 jax.ShapeDtypeStruct((batch, heads * SSM_HEAD_DIM, SSM_STATE), F32))
    out_specs = (pl.BlockSpec((tl, da + dp + ds_), lambda b, t: (b * nt + t, 0)),
                 per_batch(hist_a.shape[1:]), per_batch(hist_p.shape[1:]),
                 per_batch(hist_s.shape[1:]), per_batch((heads * SSM_HEAD_DIM, SSM_STATE)))
    scratch = [pltpu.VMEM((tl + HIST_A, da), F32), pltpu.VMEM((tl + HIST_P, dp), F32),
               pltpu.VMEM((tl + HIST_S, ds_ + 2 * nbc), F32), pltpu.VMEM((tl, ds_ + 2 * nbc), F32),
               pltpu.VMEM((tl, LANES), F32), pltpu.VMEM((q, ds_), F32),
               pltpu.VMEM((SSM_STATE, ds_), F32)]
    pre, na, np_, ns, nh = pl.pallas_call(
        functools.partial(_mixer_kernel, tl=tl, q=q, start=start, nt=nt),
        grid=(batch, nt),
        in_specs=in_specs, out_specs=out_specs, out_shape=out_shape,
        scratch_shapes=scratch,
        compiler_params=_cparams("arbitrary", "arbitrary"),
        name="mixers",
    )(proj, proj, proj, proj, proj, proj, hist_a, hist_p, hist_s,
      h0.reshape(batch, heads * SSM_HEAD_DIM, SSM_STATE), *params)
    return pre, na, np_, ns, nh.reshape(h0.shape)


def _merge_kernel(pre_ref, ga_ref, gb_ref, gc_ref, x_ref, g1_ref, sh_ref, sc_ref,
                  lg_ref, lb_ref, wa_ref, wb_ref, wc_ref, wo_ref,
                  x1_ref, h2_ref, *, alpha):
    da, dp = wa_ref.shape[0], wb_ref.shape[0]
    pre = pre_ref[...]
    mixed = jax.nn.sigmoid(ga_ref[...]) * jnp.dot(pre[:, :da], wa_ref[...], preferred_element_type=F32)
    mixed += jax.nn.sigmoid(gb_ref[...]) * jnp.dot(pre[:, da:da + dp], wb_ref[...], preferred_element_type=F32)
    mixed += jax.nn.sigmoid(gc_ref[...]) * jnp.dot(pre[:, da + dp:], wc_ref[...], preferred_element_type=F32)
    out = jnp.dot(mixed.astype(BF16), wo_ref[...], preferred_element_type=F32)
    x = _rows_from_tiles(x_ref, pre.shape[0])
    x1 = _ln(alpha * x + g1_ref[...] * out) * lg_ref[...] + lb_ref[...]
    _rows_to_tiles(x1_ref, x1)
    h2_ref[...] = (_ln(x1) * (1.0 + sc_ref[...]) + sh_ref[...]).astype(h2_ref.dtype)


def merge(pre, proj, xt, g1, sh2, sc2, ln_g, ln_b, wa, wb, wc, wo, tm, tiles_per_group, alpha):
    t = pre.shape[0]
    d = wo.shape[1]
    s = d // LANES
    row = lambda w: pl.BlockSpec((tm, w), lambda i: (i, 0))
    tiles = pl.BlockSpec((tm * s, LANES), lambda i: (i, 0))
    gate = lambda k: pl.BlockSpec((tm, d), lambda i: (i, COL_GATES // d + k))
    mods = [_mod_spec(m, tm, tiles_per_group) for m in (g1, sh2, sc2)]
    weights = [ln_g, ln_b, wa, wb, wc, wo]
    return pl.pallas_call(
        functools.partial(_merge_kernel, alpha=alpha),
        grid=(t // tm,),
        in_specs=[row(pre.shape[1]), gate(0), gate(1), gate(2), tiles] + mods
                 + [_resident(w.shape) for w in weights],
        out_specs=(tiles, row(d)),
        out_shape=(jax.ShapeDtypeStruct((t * s, LANES), F32), jax.ShapeDtypeStruct((t, d), BF16)),
        compiler_params=_cparams("arbitrary"),
        name="merge",
    )(pre, proj, proj, proj, xt, g1, sh2, sc2, *weights)


def _pack_kernel(u_ref, v_ref, o_ref):
    r, d = u_ref.shape
    s = d // LANES
    for k in range(s):
        o_ref[pl.ds(k, r, stride=2 * s), :] = u_ref[:, k * LANES:(k + 1) * LANES]
        o_ref[pl.ds(s + k, r, stride=2 * s), :] = v_ref[:, k * LANES:(k + 1) * LANES]


def pack_tables(u, v, layer, rows=256):
    _, e, d = u.shape
    per = 2 * d // LANES
    tab = pl.BlockSpec((None, rows, d), lambda i: (layer, i, 0))
    return pl.pallas_call(
        _pack_kernel,
        grid=(e // rows,),
        in_specs=[tab, tab],
        out_specs=pl.BlockSpec((rows * per, LANES), lambda i: (i, 0)),
        out_shape=jax.ShapeDtypeStruct((e * per, LANES), F32),
        compiler_params=_cparams("arbitrary"),
        name="pack_tables",
    )(u, v)


def _top_rows(s, pos, payload, k):
    big = jnp.iinfo(jnp.int32).max
    vals, pays = [], []
    for _ in range(k):
        m = jnp.max(s, axis=0, keepdims=True)
        first = jnp.min(jnp.where(s == m, pos, big), axis=0, keepdims=True)
        hit = pos == first
        vals.append(m)
        pays.append(first if payload is None else jnp.max(jnp.where(hit, payload, -1), axis=0, keepdims=True))
        s = jnp.where(hit, -jnp.inf, s)
    return jnp.concatenate(vals, axis=0), jnp.concatenate(pays, axis=0)


def _pair_candidates(kk):
    blocks = [(i, kk // (i + 1)) for i in range(kk) if kk // (i + 1) > 1]
    tail = [i for i in range(kk) if kk // (i + 1) == 1]
    return blocks, tail


def _route_kernel(q_ref, k_ref, eid_ref, gate_ref):
    nkeys, dk = k_ref.shape[1], k_ref.shape[2]
    heads = k_ref.shape[0] // 2
    kk = PEER_TOPK
    tm = q_ref.shape[0]
    key_pos = lax.broadcasted_iota(I32, (nkeys, tm), 0)
    blocks, tail = _pair_candidates(kk)
    assert tail == list(range(tail[0], kk)) and len(tail) == SUBLANES and tail[0] % SUBLANES == 0
    j_iota = lax.broadcasted_iota(I32, (kk, tm), 0)
    cpos = jnp.concatenate([j_iota + i * kk for i, _ in blocks]
                           + [(lax.broadcasted_iota(I32, (len(tail), tm), 0) + tail[0]) * kk], axis=0)
    for h in range(heads):
        side = []
        for s in range(2):
            qs = q_ref[:, pl.ds((2 * h + s) * dk, dk)].astype(BF16)
            sc = lax.dot_general(k_ref[2 * h + s], qs, (((1,), (1,)), ((), ())), preferred_element_type=F32)
            side.append(_top_rows(sc, key_pos, None, kk))
        (v1, i1), (v2, i2) = side
        cand = jnp.concatenate([jnp.where(j_iota < cnt, v2 + v1[i:i + 1, :], -jnp.inf) for i, cnt in blocks]
                               + [v1[tail[0]:, :] + v2[0:1, :]], axis=0)
        cidx = jnp.concatenate([i2 + i1[i:i + 1, :] * nkeys for i, _ in blocks]
                               + [i1[tail[0]:, :] * nkeys + i2[0:1, :]], axis=0)
        top, eid = _top_rows(cand, cpos, cidx, kk)
        e = jnp.exp(top - top[0:1, :])
        gate_ref[pl.ds(h * kk, kk), :] = e / jnp.sum(e, axis=0, keepdims=True)
        eid_ref[pl.ds(h * kk, kk), :] = eid


ROW_SUBLANES = 32
ROW_GROUP = 8


def _ln_tiles(x):
    n = x.shape[1] * x.shape[2]
    mu = jnp.sum(jnp.sum(x, axis=2, keepdims=True), axis=1, keepdims=True) / n
    xc = x - mu
    var = jnp.sum(jnp.sum(xc * xc, axis=2, keepdims=True), axis=1, keepdims=True) / n
    return xc * lax.rsqrt(var + LN_EPS)


def _sublane_sums(ms, roll, where, sub):
    masks = sub if isinstance(sub, dict) else _sublane_sum_masks(sub)

    def stage(x, y, dist, off):
        keep_x = masks[dist, off]
        return where(keep_x, x, y) + roll(where(keep_x, y, x), dist)

    halves = [stage(ms[2 * k], ms[2 * k + 1], 4, off) for k, off in enumerate((0, 2, 1, 3))]
    quarters = [stage(halves[0], halves[1], 2, 0), stage(halves[2], halves[3], 2, 1)]
    return stage(quarters[0], quarters[1], 1, 0)


def _sublane_sum_masks(sub):
    masks = {(4, off): ((sub - off) % 8) < 4 for off in range(4)}
    masks.update({(dist, off): ((sub - off - dist) % (2 * dist)) < dist for dist in (2, 1) for off in (0, 1)})
    return masks


def _sublane_sum_order():
    import numpy as np
    sub = np.arange(SUBLANES)[:, None]
    ms = [np.eye(SUBLANES)[k][None, :].repeat(SUBLANES, 0) for k in range(SUBLANES)]
    out = _sublane_sums(ms, lambda x, s: np.roll(x, s, axis=0), np.where, sub)
    assert np.array_equal(np.sort(out, axis=1)[:, :-1], np.zeros((SUBLANES, SUBLANES - 1)))
    assert np.array_equal(out.max(axis=1), np.full(SUBLANES, float(SUBLANES)))
    order = out.argmax(axis=1)
    assert sorted(order) == list(range(SUBLANES))
    return [int(k) for k in order]


def _peer_kernel(idc_ref, idn_ref, x1_ref, gate_ref, sh_ref, sc_ref, g2_ref, lg_ref, lb_ref, uv_ref,
                 o_ref, buf, sem, hs, rs, cs, ffs, *, alpha, nsteps, tb, kk):
    i = pl.program_id(0)
    slot = lax.rem(i, 2)
    rows_per_step = tb * kk
    half = ROW_SUBLANES // 2
    s_tok = x1_ref.shape[1]
    assert s_tok == half and half == 2 * SUBLANES and kk % ROW_GROUP == 0 and ROW_GROUP == SUBLANES

    def row_copy(ids_ref, s, r):
        e = ids_ref[0, r]
        src = uv_ref.at[pl.ds(pl.multiple_of(e * ROW_SUBLANES, ROW_SUBLANES), ROW_SUBLANES), :]
        dst = buf.at[s, pl.ds(pl.multiple_of(r * ROW_SUBLANES, ROW_SUBLANES), ROW_SUBLANES), :]
        return pltpu.make_async_copy(src, dst, sem.at[s])

    groups = kk // ROW_GROUP

    @pl.when(i == 0)
    def _first():
        def body(n, c):
            for k in range(ROW_GROUP):
                row_copy(idc_ref, 0, n * ROW_GROUP + k).start()
            return c
        lax.fori_loop(0, tb * groups, body, 0)

    def wait_slot(s):
        pltpu.make_async_copy(uv_ref.at[pl.ds(0, rows_per_step * ROW_SUBLANES), :], buf.at[s], sem.at[s]).wait()

    wait_slot(slot)

    hs[...] = (_ln_tiles(x1_ref[...]) * (1.0 + sc_ref[...]) + sh_ref[...]).reshape(tb * s_tok, LANES)

    sub = _sublane_sum_masks(lax.broadcasted_iota(I32, (SUBLANES, LANES), 0))
    order = _sublane_sum_order()
    slot_of = [order.index(i) for i in range(SUBLANES)]
    roll = lambda x, s: pltpu.roll(x, s, 0)

    def prefetch(n, first, count):
        for k in range(first, first + count):
            row_copy(idn_ref, 1 - slot, n * ROW_GROUP + k).start()

    pair = 2
    assert groups % pair == 0

    def dots(m, c):
        t = (m * pair) // groups
        h0 = hs[pl.ds(pl.multiple_of(t * s_tok, s_tok), SUBLANES), :]
        h1 = hs[pl.ds(pl.multiple_of(t * s_tok + SUBLANES, SUBLANES), SUBLANES), :]
        base = pl.multiple_of(m * (pair * ROW_GROUP * ROW_SUBLANES), pair * ROW_GROUP * ROW_SUBLANES)
        blk = buf[slot, pl.ds(base, pair * ROW_GROUP * ROW_SUBLANES), :]
        ms = []
        for k in range(pair * ROW_GROUP):
            u0 = blk[k * ROW_SUBLANES:k * ROW_SUBLANES + SUBLANES, :]
            u1 = blk[k * ROW_SUBLANES + SUBLANES:k * ROW_SUBLANES + half, :]
            ms.append(u0 * h0 + u1 * h1)
        for g in range(pair):
            n = m * pair + g
            rs[pl.ds(pl.multiple_of(n * ROW_GROUP, ROW_GROUP), ROW_GROUP), :] = _sublane_sums(
                [ms[g * ROW_GROUP + slot_of[i]] for i in range(ROW_GROUP)], roll, jnp.where, sub)
        for g in range(pair):
            prefetch(m * pair + g, 0, ROW_GROUP // 2)
        return c

    lax.fori_loop(0, tb * groups // pair, dots, 0)

    act = jnp.sum(rs[...], axis=1, keepdims=True)
    tok_id = lax.broadcasted_iota(I32, (tb, LANES), 0)
    for t in range(tb):
        gcol = lax.dot_general(gate_ref[...], (tok_id == t).astype(F32), (((0,), (0,)), ((), ())),
                               precision=lax.Precision.HIGHEST, preferred_element_type=F32)
        cs[pl.ds(t * kk, kk), :] = gcol * jax.nn.gelu(act[t * kk:(t + 1) * kk])

    def token(t, c):
        def axpy(j, acc):
            n = t * groups + j
            base = pl.multiple_of(n * (ROW_GROUP * ROW_SUBLANES), ROW_GROUP * ROW_SUBLANES)
            cg = cs[pl.ds(pl.multiple_of(n * ROW_GROUP, ROW_GROUP), ROW_GROUP), :]
            blk = buf[slot, pl.ds(base, ROW_GROUP * ROW_SUBLANES), :]
            vs = [(blk[k * ROW_SUBLANES + half:k * ROW_SUBLANES + half + SUBLANES, :],
                   blk[k * ROW_SUBLANES + half + SUBLANES:(k + 1) * ROW_SUBLANES, :])
                  for k in range(ROW_GROUP)]
            prefetch(n, ROW_GROUP // 2, ROW_GROUP // 2)
            acc = list(acc)
            for k in range(ROW_GROUP):
                ck = cg[k:k + 1, :]
                acc[2 * (k % 2)] = acc[2 * (k % 2)] + ck * vs[k][0]
                acc[2 * (k % 2) + 1] = acc[2 * (k % 2) + 1] + ck * vs[k][1]
            return tuple(acc)

        zero = jnp.zeros((SUBLANES, LANES), F32)
        a0, a1, b0, b1 = lax.fori_loop(0, groups, axpy, (zero, zero, zero, zero))
        r0 = pl.multiple_of(t * s_tok, s_tok)
        ffs[pl.ds(r0, SUBLANES), :] = a0 + b0
        ffs[pl.ds(r0 + SUBLANES, SUBLANES), :] = a1 + b1
        return c

    lax.fori_loop(0, tb, token, 0)

    y = alpha * x1_ref[...] + g2_ref[...] * ffs[...].reshape(tb, s_tok, LANES)
    o_ref[...] = _ln_tiles(y) * lg_ref[...] + lb_ref[...]

    @pl.when(i == nsteps - 1)
    def _drain():
        wait_slot(1 - slot)


def peer(x1, eid_t, gate_t, sh2, sc2, g2, ln_g, ln_b, uv, steps_per_group, alpha):
    t, s, _ = x1.shape
    kk = eid_t.shape[0]
    tb = PEER_TOKENS_PER_STEP
    nsteps = t // tb
    ids = eid_t.T.reshape(nsteps, 1, tb * kk)
    gates = gate_t.T.reshape(nsteps, tb, kk)

    def mod_spec(m):
        if m.shape[1] == 1:
            return pl.BlockSpec((None, 1, s, LANES), lambda i: (i // steps_per_group, 0, 0, 0))
        return pl.BlockSpec((None, tb, s, LANES), lambda i: (i, 0, 0, 0))

    tile = pl.BlockSpec((tb, s, LANES), lambda i: (i, 0, 0))
    return pl.pallas_call(
        functools.partial(_peer_kernel, alpha=alpha, nsteps=nsteps, tb=tb, kk=kk),
        grid=(nsteps,),
        in_specs=[pl.BlockSpec((None, 1, tb * kk), lambda i: (i, 0, 0), memory_space=pltpu.SMEM),
                  pl.BlockSpec((None, 1, tb * kk), lambda i: (jnp.minimum(i + 1, nsteps - 1), 0, 0),
                               memory_space=pltpu.SMEM),
                  tile,
                  pl.BlockSpec((None, tb, kk), lambda i: (i, 0, 0)),
                  mod_spec(sh2), mod_spec(sc2), mod_spec(g2),
                  _resident(ln_g.shape), _resident(ln_b.shape),
                  pl.BlockSpec(memory_space=pl.ANY)],
        out_specs=tile,
        out_shape=jax.ShapeDtypeStruct(x1.shape, F32),
        scratch_shapes=[pltpu.VMEM((2, tb * kk * ROW_SUBLANES, LANES), F32), pltpu.SemaphoreType.DMA((2,)),
                        pltpu.VMEM((tb * s, LANES), F32), pltpu.VMEM((tb * kk, LANES), F32),
                        pltpu.VMEM((tb * kk, LANES), F32), pltpu.VMEM((tb * s, LANES), F32)],
        compiler_params=_cparams("arbitrary"),
        name="peer",
    )(ids, ids, x1, gates, sh2, sc2, g2, ln_g, ln_b, uv)


def route(q, keys, tm=LANES):
    t = q.shape[0]
    rows = keys.shape[0] // 2 * PEER_TOPK
    return pl.pallas_call(
        _route_kernel,
        grid=(t // tm,),
        in_specs=[pl.BlockSpec((tm, q.shape[1]), lambda i: (i, 0)), _resident(keys.shape)],
        out_specs=(pl.BlockSpec((rows, tm), lambda i: (0, i)), pl.BlockSpec((rows, tm), lambda i: (0, i))),
        out_shape=(jax.ShapeDtypeStruct((rows, t), I32), jax.ShapeDtypeStruct((rows, t), F32)),
        compiler_params=_cparams("arbitrary"),
        name="route",
    )(q, keys)


def _tile_sizes(batch, seq):
    if seq >= 1024:
        return dict(ln=512, proj=1024, mix=256, merge=256, wq=1024, per_row_mods=False)
    return dict(ln=128, proj=batch * seq, mix=seq, merge=128, wq=batch * seq, per_row_mods=True)


def _run_layer(x, ada_rows, states, w, batch, seq, start, alpha):
    rows = batch * seq
    d = ada_rows.shape[1] // 6
    ts = _tile_sizes(batch, seq)
    vecs = jnp.split(ada_rows, 6, axis=-1)

    def mods(v, tm):
        if ts["per_row_mods"]:
            return jnp.repeat(v, seq, axis=0).reshape(rows // tm, tm, d)
        return v.reshape(batch, 1, d)

    def tile_mods(v):
        s = d // LANES
        if ts["per_row_mods"]:
            return jnp.repeat(v, seq, axis=0).reshape(rows // PEER_TOKENS_PER_STEP, PEER_TOKENS_PER_STEP, s, LANES)
        return v.reshape(batch, 1, s, LANES)

    h = ln_mod(x, mods(vecs[0], ts["ln"]), mods(vecs[1], ts["ln"]), ts["ln"], seq // ts["ln"] if seq >= ts["ln"] else 1)
    proj = matmul(h, w["w_in"], ts["proj"], PROJ_COLS // 9)
    pre, na, npool, ns, nh = mixers(proj, *states, w["mix"], batch, seq, ts["mix"], start)
    tm = ts["merge"]
    x1, h2 = merge(pre, proj, x, mods(vecs[2], tm), mods(vecs[3], tm), mods(vecs[4], tm), w["ln1_g"], w["ln1_b"],
                   w["w_a_out"], w["w_b_out"], w["w_c_out"], w["w_o"], tm, seq // tm if seq >= tm else 1, alpha)
    q = matmul(h2, w["wq"], ts["wq"], w["wq"].shape[1] // 2)
    eid_t, gate_t = route(q, w["keys"])
    s = d // LANES
    xo = peer(x1.reshape(rows, s, LANES), eid_t, gate_t, tile_mods(vecs[3]), tile_mods(vecs[4]), tile_mods(vecs[5]),
              w["ln2_g"].reshape(1, s, LANES), w["ln2_b"].reshape(1, s, LANES), w["uv"],
              seq // PEER_TOKENS_PER_STEP if not ts["per_row_mods"] else 1, alpha)
    return xo.reshape(rows * s, LANES), (na, npool, ns, nh)


def kernel(x_prompt, x_sample, state_conv_a, state_pool, state_ssm_conv, state_ssm, c_prompt, c_sample, w_ada, b_ada, w_in, conv_a_w, conv_a_b, ln_a_g, ln_a_b, w_a_out, pool_w, pool_scale, w_b_out, ssm_conv_w, ssm_conv_b, dt_bias, a_log, d_skip, ssm_norm_g, w_c_out, w_o, ln1_g, ln1_b, peer_wq, peer_keys, peer_u, peer_v, ln2_g, ln2_b):
    depth, d, _ = w_in.shape
    alpha = (2 * depth) ** 0.25
    bp, lp, _ = x_prompt.shape
    bs, ls, _ = x_sample.shape
    da, dpool, dssm = w_a_out.shape[1], w_b_out.shape[1], w_c_out.shape[1]
    nbc = SSM_GROUPS * SSM_STATE
    heads = dt_bias.shape[1]
    assert (da, dpool, dssm, d) == (512, 512, 1024, 2048) and heads * SSM_HEAD_DIM == dssm
    assert COL_A == 3 * d and PROJ_COLS == COL_DT + LANES

    n_c = bp + bs
    c_all = jnp.pad(jnp.concatenate([c_prompt, c_sample], axis=0), ((0, -n_c % SUBLANES), (0, 0)))
    ada = ada_all_layers(c_all, w_ada, b_ada)

    o_p, o_z, o_x = 2 * da, 2 * da + dpool, 2 * da + dpool + dssm
    o_bc, o_dt = o_x + dssm, o_x + dssm + 2 * nbc
    o_g = o_dt + heads

    def permute_in(wl):
        return jnp.concatenate([wl[:, o_g:], wl[:, :2 * da], wl[:, o_z:o_x], wl[:, o_x:o_bc], wl[:, o_bc:o_dt],
                                wl[:, o_p:o_z], wl[:, o_dt:o_g], jnp.zeros((d, LANES - heads), wl.dtype)], axis=1)

    lane_pad = lambda v: jnp.pad(v, (0, LANES - heads))[None]
    expand = (jnp.arange(dssm)[None, :] // SSM_HEAD_DIM == jnp.arange(LANES)[:, None]).astype(F32)

    def layer_weights(l, q):
        mix = dict(caw=conv_a_w[l], cab=conv_a_b[l][None], lag=ln_a_g[l][None], lab=ln_a_b[l][None],
                   pw=pool_w[l].astype(BF16), psc=pool_scale[l][None], scw=ssm_conv_w[l], scb=ssm_conv_b[l][None],
                   dtb=lane_pad(dt_bias[l]), alog=lane_pad(a_log[l]), dsk=jnp.repeat(d_skip[l], SSM_HEAD_DIM)[None],
                   ng=ssm_norm_g[l][None], e=expand, tril=jnp.tril(jnp.ones((q, q), F32)))
        return mix

    x_p = x_prompt.reshape(-1, LANES)
    x_s = x_sample.reshape(-1, LANES)
    zeros_p = (jnp.zeros((bp,) + state_conv_a.shape[2:], F32), jnp.zeros((bp,) + state_pool.shape[2:], F32),
               jnp.zeros((bp,) + state_ssm_conv.shape[2:], F32), jnp.zeros((bp,) + state_ssm.shape[2:], F32))
    new_p, new_s = [], []
    for l in range(depth):
        w = dict(w_in=permute_in(w_in[l]).astype(BF16), w_a_out=w_a_out[l].astype(BF16),
                 w_b_out=w_b_out[l].astype(BF16), w_c_out=w_c_out[l].astype(BF16), w_o=w_o[l].astype(BF16),
                 wq=peer_wq[l].astype(BF16), keys=peer_keys[l].reshape((-1,) + peer_keys.shape[3:]).astype(BF16),
                 uv=pack_tables(peer_u, peer_v, l),
                 ln1_g=ln1_g[l][None], ln1_b=ln1_b[l][None], ln2_g=ln2_g[l], ln2_b=ln2_b[l])
        w["mix"] = layer_weights(l, min(SSD_CHUNK, lp))
        x_p, st = _run_layer(x_p, ada[l, :bp], zeros_p, w, bp, lp, 0, alpha)
        new_p.append(st)
        w["mix"] = layer_weights(l, min(SSD_CHUNK, ls))
        x_s, st = _run_layer(x_s, ada[l, bp:n_c], (state_conv_a[l], state_pool[l], state_ssm_conv[l], state_ssm[l]),
                             w, bs, ls, PAST_LEN, alpha)
        new_s.append(st)
    stack = lambda sts, k: jnp.stack([s[k] for s in sts])
    return (x_p.reshape(x_prompt.shape), x_s.reshape(x_sample.shape),
            stack(new_p, 0), stack(new_p, 1), stack(new_p, 2), stack(new_p, 3),
            stack(new_s, 0), stack(new_s, 1), stack(new_s, 2), stack(new_s, 3))
```
